```python
import math
import jax, jax.numpy as jnp
from jax import lax
import numpy as np

D_MODEL = 1024
BATCH = 8
SEQ = 2048
DEPTH = 2
DEC_BATCH = 32
DEC_SEQ = 4
PAST_LEN = 16384
PAGE_SIZE = 128

N_A_LAYERS = DEPTH // 2
N_B_LAYERS = DEPTH - N_A_LAYERS
POOL_WINDOWS = (2, 4, 8, 16)
N_POOL_GROUPS = len(POOL_WINDOWS)
POOL_GROUP_DIM = D_MODEL // N_POOL_GROUPS
POOL_STATE_ROWS = max(POOL_WINDOWS) - 1
HEAD_DIM = 64
N_HEADS = D_MODEL // (2 * HEAD_DIM)
V_DIM = 2 * HEAD_DIM
ATTN_WIDTH = N_HEADS * V_DIM
D_FF = ((8 * D_MODEL + 3 * 256 - 1) // (3 * 256)) * 256
Q_BLOCK = 128
NORM_EPS = 1e-6
SUBLN_EPS = 1e-5
NEG_INF = -1e30

kernel_name = "yoco_pool_diffattn_alibi_step"


def rmsnorm(x, g, eps=NORM_EPS):
    xf = x.astype(jnp.float32)
    y = xf * lax.rsqrt(jnp.mean(xf * xf, axis=-1, keepdims=True) + eps)
    return (y * g.astype(jnp.float32)).astype(x.dtype)


def alibi_slopes():
    return jnp.exp2(-8.0 * (jnp.arange(N_HEADS, dtype=jnp.float32) + 1.0) / N_HEADS)


def swiglu(h, w_gate, w_up, w_down):
    return (jax.nn.silu(h @ w_gate) * (h @ w_up)) @ w_down


def pool_mixer(h, prev, w_pool, pool_scale, pos0):
    B, T, D = h.shape
    P = prev.shape[1]
    ext = jnp.concatenate([prev.astype(h.dtype), h], axis=1)
    c = jnp.cumsum(ext.astype(jnp.float32), axis=1)
    c = jnp.pad(c, ((0, 0), (1, 0), (0, 0)))
    end = c[:, P + 1:P + 1 + T]
    pos = pos0 + jnp.arange(T)
    means = []
    for g, w in enumerate(POOL_WINDOWS):
        sl = slice(g * POOL_GROUP_DIM, (g + 1) * POOL_GROUP_DIM)
        start = c[:, P + 1 - w:P + 1 - w + T, sl]
        cnt = jnp.minimum(w, pos + 1).astype(jnp.float32)[None, :, None]
        means.append((end[..., sl] - start) / cnt)
    pooled = jnp.stack(means, axis=2)
    diff = (pooled - h.reshape(B, T, N_POOL_GROUPS, POOL_GROUP_DIM).astype(jnp.float32)).astype(h.dtype)
    y = jnp.einsum('btgc,gce->btge', diff, w_pool).reshape(B, T, D) * pool_scale
    return y, ext[:, ext.shape[1] - P:]


def chunk_stats(q, k, v, bias, mask):
    B, Tk = k.shape[0], k.shape[1]
    kk = k.reshape(B, Tk, N_HEADS, 2, HEAD_DIM)
    s = jnp.einsum('bqhmd,bkhmd->bhmqk', q, kk, preferred_element_type=jnp.float32)
    s = jnp.where(mask, s + bias[None, :, None], NEG_INF)
    m = jnp.max(s, axis=-1)
    p = jnp.where(mask, jnp.exp(s - m[..., None]), 0.0)
    l = jnp.sum(p, axis=-1)
    acc = jnp.einsum('bhmqk,bkhe->bhmqe', p.astype(v.dtype), v, preferred_element_type=jnp.float32)
    return m, l, acc


def merge_stats(a, b):
    ma, la, acca = a
    mb, lb, accb = b
    m = jnp.maximum(ma, mb)
    ea, eb = jnp.exp(ma - m), jnp.exp(mb - m)
    return m, la * ea + lb * eb, acca * ea[..., None] + accb * eb[..., None]


def diff_finalize(stats, lam, lam_init, g_sub, out_dtype):
    m, l, acc = stats
    o = acc / l[..., None]
    o = o[:, :, 0] - lam * o[:, :, 1]
    o = rmsnorm(o, g_sub, SUBLN_EPS) * (1.0 - lam_init)
    B, H, Tq, V = o.shape
    return o.transpose(0, 2, 1, 3).reshape(B, Tq, H * V).astype(out_dtype)


def make_prompt_attend(slopes):
    def attend(q, k, v, lam, lam_init, g_sub):
        B, S = q.shape[0], q.shape[1]
        nb = S // Q_BLOCK
        qb = q.reshape(B, nb, Q_BLOCK, N_HEADS, 2, HEAD_DIM).transpose(1, 0, 2, 3, 4, 5)
        kpos = jnp.arange(S)

        def one(args):
            qblk, i = args
            qpos = i * Q_BLOCK + jnp.arange(Q_BLOCK)
            rel = qpos[:, None] - kpos[None, :]
            bias = -slopes[:, None, None] * rel.astype(jnp.float32)[None]
            st = chunk_stats(qblk, k, v, bias, rel >= 0)
            return diff_finalize(st, lam, lam_init, g_sub, q.dtype)

        out = lax.map(one, (qb, jnp.arange(nb)))
        return out.transpose(1, 0, 2, 3).reshape(B, S, ATTN_WIDTH)
    return attend


def make_sample_attend(slopes, cache_k, cache_v, page_table):
    def attend(q, k_new, v_new, lam, lam_init, g_sub):
        B, Td = q.shape[0], q.shape[1]
        n_pages = page_table.shape[1]
        qpos = n_pages * PAGE_SIZE + jnp.arange(Td)
        init = (jnp.full((B, N_HEADS, 2, Td), NEG_INF, jnp.float32),
                jnp.zeros((B, N_HEADS, 2, Td), jnp.float32),
                jnp.zeros((B, N_HEADS, 2, Td, V_DIM), jnp.float32))

        def step(carry, p):
            phys = page_table[:, p]
            kp, vp = cache_k[phys], cache_v[phys]
            kpos = p * PAGE_SIZE + jnp.arange(PAGE_SIZE)
            rel = qpos[:, None] - kpos[None, :]
            bias = -slopes[:, None, None] * rel.astype(jnp.float32)[None]
            return merge_stats(carry, chunk_stats(q, kp, vp, bias, rel >= 0)), None

        carry, _ = lax.scan(step, init, jnp.arange(n_pages))
        rel = qpos[:, None] - qpos[None, :]
        bias = -slopes[:, None, None] * rel.astype(jnp.float32)[None]
        st = merge_stats(carry, chunk_stats(q, k_new, v_new, bias, rel >= 0))
        return diff_finalize(st, lam, lam_init, g_sub, q.dtype)
    return attend


def trunk(x, pool_prev, pos0, attend, p):
    B, T, D = x.shape
    new_pool = []
    k = v = None
    for layer in range(DEPTH):
        if layer < N_A_LAYERS:
            h = rmsnorm(x, p['g_mix_pre'][layer])
            y, st = pool_mixer(h, pool_prev[layer], p['w_pool'][layer], p['pool_scale'][layer], pos0)
            new_pool.append(st)
        else:
            j = layer - N_A_LAYERS
            if j == 0:
                src = rmsnorm(x, p['g_kv'])
                k = (src @ p['w_k']).reshape(B, T, N_HEADS, V_DIM)
                v = (src @ p['w_v']).reshape(B, T, N_HEADS, V_DIM)
            h = rmsnorm(x, p['g_mix_pre'][layer])
            q = (h @ p['w_q'][j]).reshape(B, T, N_HEADS, 2, HEAD_DIM) * (HEAD_DIM ** -0.5)
            lq = p['lambda_qk'][j].astype(jnp.float32)
            lam_init = 0.8 - 0.6 * math.exp(-0.3 * layer)
            lam = jnp.exp(jnp.sum(lq[0] * lq[1])) - jnp.exp(jnp.sum(lq[2] * lq[3])) + lam_init
            o = attend(q, k, v, lam, lam_init, p['g_sub'][j])
            y = o @ p['w_o'][j]
        x = x + rmsnorm(y, p['g_mix_post'][layer])
        h = rmsnorm(x, p['g_ffn_pre'][layer])
        x = x + rmsnorm(swiglu(h, p['w_gate'][layer], p['w_up'][layer], p['w_down'][layer]), p['g_ffn_post'][layer])
    return x, jnp.stack(new_pool, axis=0), k, v


def setup_inputs(seed: int = 0) -> dict:
    key = jax.random.key(seed)
    ks = jax.random.split(key, 24)
    f32 = jnp.float32
    n_pages = PAST_LEN // PAGE_SIZE
    n_used = DEC_BATCH * n_pages
    n_pool = n_used + n_used // 4

    def nrm(k, shape, scale):
        return jax.random.normal(k, shape, f32) * scale

    def gain(k, shape):
        return 1.0 + 0.05 * jax.random.normal(k, shape, f32)

    perm = jax.random.permutation(ks[5], n_pool)
    page_table = perm[:n_used].reshape(DEC_BATCH, n_pages).astype(jnp.int32)
    return {
        'x_prompt': nrm(ks[0], (BATCH, SEQ, D_MODEL), 1.0),
        'x_sample': nrm(ks[1], (DEC_BATCH, DEC_SEQ, D_MODEL), 1.0),
        'state_pool': nrm(ks[2], (N_A_LAYERS, DEC_BATCH, POOL_STATE_ROWS, D_MODEL), 1.0),
        'cache_k': nrm(ks[3], (n_pool, PAGE_SIZE, N_HEADS, V_DIM), 1.0),
        'cache_v': nrm(ks[4], (n_pool, PAGE_SIZE, N_HEADS, V_DIM), 1.0),
        'page_table': page_table,
        'w_pool': nrm(ks[6], (N_A_LAYERS, N_POOL_GROUPS, POOL_GROUP_DIM, POOL_GROUP_DIM), POOL_GROUP_DIM ** -0.5),
        'pool_scale': gain(ks[7], (N_A_LAYERS, D_MODEL)),
        'g_kv': gain(ks[8], (D_MODEL,)),
        'w_k': nrm(ks[9], (D_MODEL, ATTN_WIDTH), D_MODEL ** -0.5),
        'w_v': nrm(ks[10], (D_MODEL, ATTN_WIDTH), D_MODEL ** -0.5),
        'w_q': nrm(ks[11], (N_B_LAYERS, D_MODEL, ATTN_WIDTH), D_MODEL ** -0.5),
        'w_o': nrm(ks[12], (N_B_LAYERS, ATTN_WIDTH, D_MODEL), ATTN_WIDTH ** -0.5),
        'lambda_qk': nrm(ks[13], (N_B_LAYERS, 4, HEAD_DIM), 0.1),
        'g_sub': gain(ks[14], (N_B_LAYERS, V_DIM)),
        'g_mix_pre': gain(ks[15], (DEPTH, D_MODEL)),
        'g_mix_post': gain(ks[16], (DEPTH, D_MODEL)),
        'g_ffn_pre': gain(ks[17], (DEPTH, D_MODEL)),
        'g_ffn_post': gain(ks[18], (DEPTH, D_MODEL)),
        'w_gate': nrm(ks[19], (DEPTH, D_MODEL, D_FF), D_MODEL ** -0.5),
        'w_up': nrm(ks[20], (DEPTH, D_MODEL, D_FF), D_MODEL ** -0.5),
        'w_down': nrm(ks[21], (DEPTH, D_FF, D_MODEL), D_FF ** -0.5),
    }


def reference(x_prompt, x_sample, state_pool, cache_k, cache_v, page_table,
              w_pool, pool_scale, g_kv, w_k, w_v, w_q, w_o, lambda_qk, g_sub,
              g_mix_pre, g_mix_post, g_ffn_pre, g_ffn_post, w_gate, w_up, w_down):
    params = dict(w_pool=w_pool, pool_scale=pool_scale, g_kv=g_kv, w_k=w_k, w_v=w_v,
                  w_q=w_q, w_o=w_o, lambda_qk=lambda_qk, g_sub=g_sub,
                  g_mix_pre=g_mix_pre, g_mix_post=g_mix_post, g_ffn_pre=g_ffn_pre,
                  g_ffn_post=g_ffn_post, w_gate=w_gate, w_up=w_up, w_down=w_down)
    slopes = alibi_slopes()
    past_len = page_table.shape[1] * PAGE_SIZE

    pool_prev_prompt = jnp.zeros((N_A_LAYERS, x_prompt.shape[0], POOL_STATE_ROWS, D_MODEL), x_prompt.dtype)
    y_prompt, pool_state_prompt, k_prompt, v_prompt = trunk(
        x_prompt, pool_prev_prompt, 0, make_prompt_attend(slopes), params)

    y_sample, pool_state_sample, k_sample, v_sample = trunk(
        x_sample, state_pool, past_len,
        make_sample_attend(slopes, cache_k, cache_v, page_table), params)

    return (y_prompt, y_sample, pool_state_prompt, pool_state_sample, k_prompt, v_prompt, k_sample, v_sample)
```

```python
import functools
import math

import jax
import jax.numpy as jnp
from jax import lax
from jax.experimental import pallas as pl
from jax.experimental.pallas import tpu as pltpu

F32 = jnp.float32
BF16 = jnp.bfloat16

D_MODEL = 1024
POOL_WINDOWS = (2, 4, 8, 16)
POOL_GROUP_DIM = D_MODEL // len(POOL_WINDOWS)
POOL_STATE_ROWS = max(POOL_WINDOWS) - 1
HALO_ROWS = POOL_STATE_ROWS + 1
HEAD_DIM = 64
V_DIM = 2 * HEAD_DIM
N_HEADS = D_MODEL // V_DIM
PAGE_SIZE = 128
PAGE_ROWS = PAGE_SIZE * N_HEADS
NORM_EPS = 1e-6
SUBLN_EPS = 1e-5
NEG_INF = -1e30
N_A_LAYERS = 1

VMEM_LIMIT_BYTES = 56 * 1024 * 1024
TOKEN_TILE = 512
ATTN_BLOCK = 256
PAGES_PER_STEP = 4
NEW_KEY_PAD = 16


def _rms(xf, g, eps):
    return xf * lax.rsqrt(jnp.mean(xf * xf, axis=-1, keepdims=True) + eps) * g


def _const_spec(shape):
    zeros = (0,) * len(shape)
    return pl.BlockSpec(shape, lambda *_: zeros, pipeline_mode=pl.Buffered(1))


def _log2(n):
    assert n & (n - 1) == 0, n
    return n.bit_length() - 1


def _mixer_body(x_ref, prev_ref, wpool_ref, pscale_ref, gpre_ref, gpost_ref,
                out_ref, state_ref, ext_ref, y_ref, *, tt, n_tiles, pos0):
    j = pl.program_id(1)

    @pl.when(j == 0)
    def _():
        ext_ref[0:1, :] = jnp.zeros((1, D_MODEL), F32)
        ext_ref[1:HALO_ROWS, :] = prev_ref[0]

    x = x_ref[0]
    h = _rms(x, gpre_ref[...], NORM_EPS)
    ext_ref[HALO_ROWS:HALO_ROWS + tt, :] = h
    pos = pos0 + j * tt + lax.broadcasted_iota(jnp.int32, (tt, 1), 0)
    for g, w in enumerate(POOL_WINDOWS):
        cols = slice(g * POOL_GROUP_DIM, (g + 1) * POOL_GROUP_DIM)
        e = h[:, cols]
        s = e
        for k in range(1, w):
            s = s + ext_ref[HALO_ROWS - k:HALO_ROWS - k + tt, cols]
        inv_cnt = 1.0 / jnp.minimum(w, pos + 1).astype(F32)
        diff = (s * inv_cnt - e).astype(BF16)
        yg = jnp.dot(diff, wpool_ref[g], preferred_element_type=F32)
        y_ref[:, cols] = yg * pscale_ref[:, cols]
    out_ref[0] = x + _rms(y_ref[...], gpost_ref[...], NORM_EPS)

    @pl.when(j == n_tiles - 1)
    def _():
        state_ref[0] = ext_ref[tt + 1:tt + HALO_ROWS, :]

    if n_tiles > 1:
        ext_ref[0:HALO_ROWS, :] = ext_ref[tt:tt + HALO_ROWS, :]


def _mixer(x, prev, wpool, pscale, gpre, gpost, *, pos0, tt):
    b, t, d = x.shape
    n_tiles = t // tt
    body = functools.partial(_mixer_body, tt=tt, n_tiles=n_tiles, pos0=pos0)
    return pl.pallas_call(
        body,
        grid=(b, n_tiles),
        in_specs=[
            pl.BlockSpec((1, tt, d), lambda i, j: (i, j, 0)),
            pl.BlockSpec((1, POOL_STATE_ROWS, d), lambda i, j: (i, 0, 0)),
            _const_spec(wpool.shape),
            _const_spec((1, d)),
            _const_spec((1, d)),
            _const_spec((1, d)),
        ],
        out_specs=[
            pl.BlockSpec((1, tt, d), lambda i, j: (i, j, 0)),
            pl.BlockSpec((1, POOL_STATE_ROWS, d), lambda i, j: (i, 0, 0)),
        ],
        out_shape=[
            jax.ShapeDtypeStruct((b, t, d), F32),
            jax.ShapeDtypeStruct((b, POOL_STATE_ROWS, d), F32),
        ],
        scratch_shapes=[
            pltpu.VMEM((HALO_ROWS + tt, d), F32),
            pltpu.VMEM((tt, d), F32),
        ],
        compiler_params=pltpu.CompilerParams(
            dimension_semantics=("arbitrary", "arbitrary"),
            vmem_limit_bytes=VMEM_LIMIT_BYTES),
        name="pool_mixer",
    )(x, prev, wpool, pscale, gpre, gpost)


def _ffn_body(*refs, with_attn, ff_chunks):
    if with_attn:
        (x_ref, o_ref, wo_ref, gmix_ref, gpre_ref, gpost_ref, wg_ref, wu_ref, wd_ref,
         out_ref, act_ref) = refs
    else:
        x_ref, gpre_ref, gpost_ref, wg_ref, wu_ref, wd_ref, out_ref, act_ref = refs
    x = x_ref[...]
    if with_attn:
        y = jnp.dot(o_ref[...].astype(BF16), wo_ref[...], preferred_element_type=F32)
        x = x + _rms(y, gmix_ref[...], NORM_EPS)
    h = _rms(x, gpre_ref[...], NORM_EPS).astype(BF16)
    for c0, c1 in ff_chunks:
        gate = jnp.dot(h, wg_ref[:, c0:c1], preferred_element_type=F32)
        up = jnp.dot(h, wu_ref[:, c0:c1], preferred_element_type=F32)
        act_ref[:, c0:c1] = (gate * jax.nn.sigmoid(gate) * up).astype(BF16)
    f = jnp.dot(act_ref[...], wd_ref[...], preferred_element_type=F32)
    out_ref[...] = x + _rms(f, gpost_ref[...], NORM_EPS)


def _ffn(x, gpre, gpost, wg, wu, wd, *, tm, attn=None):
    n, d = x.shape
    d_ff = wg.shape[1]
    half = d_ff // 2
    ff_chunks = ((0, half), (half, d_ff)) if half % 128 == 0 else ((0, d_ff),)
    row_spec = pl.BlockSpec((tm, d), lambda i: (i, 0))
    args = [x]
    in_specs = [row_spec]
    if attn is not None:
        o, wo, gmix = attn
        args += [o, wo, gmix]
        in_specs += [pl.BlockSpec((tm, o.shape[1]), lambda i: (i, 0)),
                     _const_spec(wo.shape), _const_spec((1, d))]
    args += [gpre, gpost, wg, wu, wd]
    in_specs += [_const_spec((1, d)), _const_spec((1, d)),
                 _const_spec(wg.shape), _const_spec(wu.shape), _const_spec(wd.shape)]
    body = functools.partial(_ffn_body, with_attn=attn is not None, ff_chunks=ff_chunks)
    return pl.pallas_call(
        body,
        grid=(n // tm,),
        in_specs=in_specs,
        out_specs=row_spec,
        out_shape=jax.ShapeDtypeStruct((n, d), F32),
        scratch_shapes=[pltpu.VMEM((tm, d_ff), BF16)],
        compiler_params=pltpu.CompilerParams(
            dimension_semantics=("arbitrary",),
            vmem_limit_bytes=VMEM_LIMIT_BYTES),
        name="attn_out_ffn" if attn is not None else "ffn",
    )(*args)


def _proj_body(x_ref, gkv_ref, gq_ref, wk_ref, wv_ref, wq_ref,
               k_ref, v_ref, q_ref, kb_ref, vb_ref, qb_ref):
    x = x_ref[...]
    src = _rms(x, gkv_ref[...], NORM_EPS).astype(BF16)
    k = jnp.dot(src, wk_ref[...], preferred_element_type=F32)
    k_ref[...] = k
    kb_ref[...] = k.astype(BF16)
    v = jnp.dot(src, wv_ref[...], preferred_element_type=F32)
    v_ref[...] = v
    vb_ref[...] = v.astype(BF16)
    h = _rms(x, gq_ref[...], NORM_EPS).astype(BF16)
    q = jnp.dot(h, wq_ref[...], preferred_element_type=F32) * (HEAD_DIM ** -0.5)
    q_ref[...] = q
    qb_ref[...] = q.astype(BF16)


def _proj(x, gkv, gq, wk, wv, wq, *, tm):
    n, d = x.shape
    w = wk.shape[1]
    row_in = pl.BlockSpec((tm, d), lambda i: (i, 0))
    row_out = pl.BlockSpec((tm, w), lambda i: (i, 0))
    return pl.pallas_call(
        _proj_body,
        grid=(n // tm,),
        in_specs=[row_in, _const_spec((1, d)), _const_spec((1, d)),
                  _const_spec(wk.shape), _const_spec(wv.shape), _const_spec(wq.shape)],
        out_specs=[row_out] * 6,
        out_shape=[jax.ShapeDtypeStruct((n, w), F32)] * 3 + [jax.ShapeDtypeStruct((n, w), BF16)] * 3,
        compiler_params=pltpu.CompilerParams(
            dimension_semantics=("arbitrary",),
            vmem_limit_bytes=VMEM_LIMIT_BYTES),
        name="kvq_proj",
    )(x, gkv, gq, wk, wv, wq)


def _lambda(lq_ref, lam_init):
    lq = lq_ref[...]
    a = jnp.sum(lq[0:1] * lq[1:2], axis=-1, keepdims=True)
    b = jnp.sum(lq[2:3] * lq[3:4], axis=-1, keepdims=True)
    return jnp.exp(a) - jnp.exp(b) + lam_init


def _diff_finalize(o1, o2, lq_ref, gsub_ref, lam_init):
    od = o1 - _lambda(lq_ref, lam_init) * o2
    return _rms(od, gsub_ref[...], SUBLN_EPS) * (1.0 - lam_init)


def _online_update(m_ref, l_ref, acc_ref, m_cur):
    m_prev = m_ref[...]
    m_new = jnp.maximum(m_prev, m_cur)
    alpha = jnp.exp(m_prev - m_new)
    m_ref[...] = m_new
    l_ref[...] = alpha * l_ref[...]
    acc_ref[...] = alpha * acc_ref[...]
    return m_new


def _prompt_attn_body(slopes_ref, q_ref, k_ref, v_ref, lq_ref, gsub_ref, o_ref,
                      bias_ref, biasd_ref, m_ref, l_ref, acc_ref, *, blk, lam_init):
    head = pl.program_id(1)
    i = pl.program_id(2)
    neg_slope = -slopes_ref[head]

    @pl.when(i == 0)
    def _():
        rel = (lax.broadcasted_iota(jnp.int32, (blk, blk), 0)
               - lax.broadcasted_iota(jnp.int32, (blk, blk), 1)).astype(F32)
        bias = neg_slope * rel
        bias_ref[...] = bias
        biasd_ref[...] = jnp.where(rel >= 0.0, bias, NEG_INF)

    q = q_ref[...]
    lane = lax.broadcasted_iota(jnp.int32, q.shape, 1)
    zero = jnp.zeros_like(q)
    q2 = jnp.concatenate([jnp.where(lane < HEAD_DIM, q, zero),
                          jnp.where(lane >= HEAD_DIM, q, zero)], axis=0)
    m_ref[...] = jnp.full(m_ref.shape, NEG_INF, F32)
    l_ref[...] = jnp.zeros(l_ref.shape, F32)
    acc_ref[...] = jnp.zeros(acc_ref.shape, F32)

    def block(j, bias, offset):
        start = pl.multiple_of(j * blk, blk)
        kb = k_ref[pl.ds(start, blk), :]
        vb = v_ref[pl.ds(start, blk), :]
        s = lax.dot_general(q2, kb, (((1,), (1,)), ((), ())), preferred_element_type=F32)
        t = (s.reshape(2, blk, blk) + bias[None]).reshape(2 * blk, blk)
        m_cur = jnp.max(t, axis=-1, keepdims=True) + offset
        m_new = _online_update(m_ref, l_ref, acc_ref, m_cur)
        p = jnp.exp(t - (m_new - offset))
        l_ref[...] += jnp.sum(p, axis=-1, keepdims=True)
        acc_ref[...] += jnp.dot(p.astype(BF16), vb, preferred_element_type=F32)

    def past_block(j, carry):
        block(j, bias_ref[...], neg_slope * ((i - j) * blk).astype(F32))
        return carry

    lax.fori_loop(0, i, past_block, 0)
    block(i, biasd_ref[...], 0.0)

    o = acc_ref[...] / l_ref[...]
    o_ref[...] = _diff_finalize(o[:blk], o[blk:], lq_ref, gsub_ref, lam_init).astype(o_ref.dtype)


def _prompt_attention(qb, kb, vb, slopes, lq, gsub, *, batch, seq, lam_init):
    blk = ATTN_BLOCK
    nq = seq // blk
    body = functools.partial(_prompt_attn_body, blk=blk, lam_init=lam_init)
    grid_spec = pltpu.PrefetchScalarGridSpec(
        num_scalar_prefetch=1,
        grid=(batch, N_HEADS, nq),
        in_specs=[
            pl.BlockSpec((blk, V_DIM), lambda b, h, i, s: (b * nq + i, h)),
            pl.BlockSpec((seq, V_DIM), lambda b, h, i, s: (b, h)),
            pl.BlockSpec((seq, V_DIM), lambda b, h, i, s: (b, h)),
            pl.BlockSpec(lq.shape, lambda b, h, i, s: (0, 0)),
            pl.BlockSpec(gsub.shape, lambda b, h, i, s: (0, 0)),
        ],
        out_specs=pl.BlockSpec((blk, V_DIM), lambda b, h, i, s: (b * nq + i, h)),
        scratch_shapes=[
            pltpu.VMEM((blk, blk), F32),
            pltpu.VMEM((blk, blk), F32),
            pltpu.VMEM((2 * blk, 1), F32),
            pltpu.VMEM((2 * blk, 1), F32),
            pltpu.VMEM((2 * blk, V_DIM), F32),
        ],
    )
    return pl.pallas_call(
        body,
        grid_spec=grid_spec,
        out_shape=jax.ShapeDtypeStruct(qb.shape, BF16),
        compiler_params=pltpu.CompilerParams(
            dimension_semantics=("arbitrary", "arbitrary", "arbitrary"),
            vmem_limit_bytes=VMEM_LIMIT_BYTES),
        name="prompt_diff_attention",
    )(slopes, qb, kb, vb, lq, gsub)


def _sample_attn_body(pt_ref, qall_ref, slope_ref, *rest, pages_per_step, n_steps, past_len,
                      dec_seq, lam_init):
    k_refs = rest[:pages_per_step]
    v_refs = rest[pages_per_step:2 * pages_per_step]
    (knew_ref, vnew_ref, lq_ref, gsub_ref, o_ref,
     bias_ref, m_ref, l_ref, acc_ref) = rest[2 * pages_per_step:]
    del pt_ref
    step = pl.program_id(1)
    n_rows = qall_ref.shape[1]
    slope = slope_ref[:, 0:1]
    contract_last = (((1,), (1,)), ((), ()))

    def row_col_ids(shape):
        r = lax.broadcasted_iota(jnp.int32, shape, 0)
        c = lax.broadcasted_iota(jnp.int32, shape, 1)
        head_q = (r >> _log2(dec_seq)) & (N_HEADS - 1)
        return head_q, r & (dec_seq - 1), c >> _log2(N_HEADS), c & (N_HEADS - 1)

    @pl.when(step == 0)
    def _():
        hq, tok, key, hk = row_col_ids((n_rows, PAGE_ROWS))
        bias_ref[...] = jnp.where(hq == hk, -slope * (tok - key).astype(F32), NEG_INF)
        m_ref[...] = jnp.full(m_ref.shape, NEG_INF, F32)
        l_ref[...] = jnp.zeros(l_ref.shape, F32)
        acc_ref[...] = jnp.zeros(acc_ref.shape, F32)

    q = qall_ref[0]
    bias = bias_ref[...]
    ts, offs = [], []
    m_cur = None
    for i in range(pages_per_step):
        s = lax.dot_general(q, k_refs[i][...], contract_last, preferred_element_type=F32)
        t = s + bias
        page_pos = (step * pages_per_step + i) * PAGE_SIZE
        off = -slope * (past_len - page_pos).astype(F32)
        m_i = jnp.max(t, axis=-1, keepdims=True) + off
        m_cur = m_i if m_cur is None else jnp.maximum(m_cur, m_i)
        ts.append(t)
        offs.append(off)
    m_new = _online_update(m_ref, l_ref, acc_ref, m_cur)
    l_add = jnp.zeros(l_ref.shape, F32)
    acc_add = jnp.zeros(acc_ref.shape, F32)
    for i in range(pages_per_step):
        p = jnp.exp(ts[i] - (m_new - offs[i]))
        l_add = l_add + jnp.sum(p, axis=-1, keepdims=True)
        acc_add = acc_add + jnp.dot(p, v_refs[i][...], preferred_element_type=F32)
    l_ref[...] += l_add
    acc_ref[...] += acc_add

    @pl.when(step == n_steps - 1)
    def _():
        hq, tok, key, hk = row_col_ids((n_rows, knew_ref.shape[1]))
        valid = (hq == hk) & (key <= tok)
        bias_new = jnp.where(valid, -slope * (tok - key).astype(F32), NEG_INF)
        s = lax.dot_general(q, knew_ref[0], contract_last, preferred_element_type=F32)
        t = s + bias_new
        m_fin = _online_update(m_ref, l_ref, acc_ref, jnp.max(t, axis=-1, keepdims=True))
        p = jnp.exp(t - m_fin)
        l = l_ref[...] + jnp.sum(p, axis=-1, keepdims=True)
        acc = acc_ref[...] + jnp.dot(p, vnew_ref[0], preferred_element_type=F32)
        o = acc / l
        half = n_rows // 2
        o_ref[0] = _diff_finalize(o[:half], o[half:], lq_ref, gsub_ref, lam_init)


def _page_index(b, s, pt, *, i, n_pages, pages_per_step):
    return (pt[b * n_pages + s * pages_per_step + i], 0)


def _sample_attention(q_s, k_s, v_s, cache_k, cache_v, page_table, slopes, lq, gsub, *, lam_init):
    dec_batch, dec_seq, _ = q_s.shape
    n_pages = page_table.shape[1]
    pps = PAGES_PER_STEP
    n_steps = n_pages // pps
    n_rows = 2 * N_HEADS * dec_seq

    q5 = q_s.reshape(dec_batch, dec_seq, N_HEADS, 2, HEAD_DIM)
    zeros = jnp.zeros_like(q5[..., 0, :])
    qm = jnp.stack([jnp.concatenate([q5[..., 0, :], zeros], axis=-1),
                    jnp.concatenate([zeros, q5[..., 1, :]], axis=-1)], axis=1)
    qall = qm.transpose(0, 1, 3, 2, 4).reshape(dec_batch, n_rows, V_DIM)
    row_slope = jnp.broadcast_to(jnp.tile(jnp.repeat(slopes, dec_seq), 2)[:, None], (n_rows, 128))

    def pad_new(a):
        a = a.reshape(dec_batch, dec_seq, N_HEADS, V_DIM)
        a = jnp.pad(a, ((0, 0), (0, NEW_KEY_PAD - dec_seq), (0, 0), (0, 0)))
        return a.reshape(dec_batch, NEW_KEY_PAD * N_HEADS, V_DIM)

    knew, vnew = pad_new(k_s), pad_new(v_s)
    ck = cache_k.reshape(-1, V_DIM)
    cv = cache_v.reshape(-1, V_DIM)
    page_specs = [
        pl.BlockSpec((PAGE_ROWS, V_DIM),
                     functools.partial(_page_index, i=i, n_pages=n_pages, pages_per_step=pps))
        for i in range(pps)]
    body = functools.partial(_sample_attn_body, pages_per_step=pps, n_steps=n_steps,
                             past_len=n_pages * PAGE_SIZE, dec_seq=dec_seq, lam_init=lam_init)
    grid_spec = pltpu.PrefetchScalarGridSpec(
        num_scalar_prefetch=1,
        grid=(dec_batch, n_steps),
        in_specs=[
            pl.BlockSpec((1, n_rows, V_DIM), lambda b, s, pt: (b, 0, 0)),
            pl.BlockSpec(row_slope.shape, lambda b, s, pt: (0, 0)),
            *page_specs, *page_specs,
            pl.BlockSpec((1,) + knew.shape[1:], lambda b, s, pt: (b, 0, 0)),
            pl.BlockSpec((1,) + vnew.shape[1:], lambda b, s, pt: (b, 0, 0)),
            pl.BlockSpec(lq.shape, lambda b, s, pt: (0, 0)),
            pl.BlockSpec(gsub.shape, lambda b, s, pt: (0, 0)),
        ],
        out_specs=pl.BlockSpec((1, n_rows // 2, V_DIM), lambda b, s, pt: (b, 0, 0)),
        scratch_shapes=[
            pltpu.VMEM((n_rows, PAGE_ROWS), F32),
            pltpu.VMEM((n_rows, 1), F32),
            pltpu.VMEM((n_rows, 1), F32),
            pltpu.VMEM((n_rows, V_DIM), F32),
        ],
    )
    o = pl.pallas_call(
        body,
        grid_spec=grid_spec,
        out_shape=jax.ShapeDtypeStruct((dec_batch, n_rows // 2, V_DIM), F32),
        compiler_params=pltpu.CompilerParams(
            dimension_semantics=("arbitrary", "arbitrary"),
            vmem_limit_bytes=VMEM_LIMIT_BYTES),
        name="sample_paged_diff_attention",
    )(page_table.reshape(-1), qall, row_slope, *([ck] * pps), *([cv] * pps), knew, vnew, lq, gsub)
    o = o.reshape(dec_batch, N_HEADS, dec_seq, V_DIM).transpose(0, 2, 1, 3)
    return o.reshape(dec_batch * dec_seq, N_HEADS * V_DIM)


def kernel(x_prompt, x_sample, state_pool, cache_k, cache_v, page_table, w_pool, pool_scale, g_kv,
           w_k, w_v, w_q, w_o, lambda_qk, g_sub, g_mix_pre, g_mix_post, g_ffn_pre, g_ffn_post,
           w_gate, w_up, w_down):
    batch, seq, d = x_prompt.shape
    dec_batch, dec_seq, _ = x_sample.shape
    past_len = page_table.shape[1] * PAGE_SIZE
    depth = g_mix_pre.shape[0]
    assert depth == 2 and w_pool.shape[0] == N_A_LAYERS and d == D_MODEL

    row = lambda a: a.reshape(1, -1)
    wpool = w_pool[0].astype(BF16)
    wg, wu, wd = (w.astype(BF16) for w in (w_gate, w_up, w_down))
    wk, wv, wq, wo = (w.astype(BF16) for w in (w_k, w_v, w_q[0], w_o[0]))
    slopes = jnp.exp2(-8.0 * (jnp.arange(N_HEADS, dtype=F32) + 1.0) / N_HEADS)
    lam_init = 0.8 - 0.6 * math.exp(-0.3 * 1)
    lq = lambda_qk[0].astype(F32)
    gsub = row(g_sub[0])

    def layer0(x, prev, pos0, tt, tm):
        b, t, _ = x.shape
        x1, state = _mixer(x, prev, wpool, row(pool_scale[0]), row(g_mix_pre[0]),
                           row(g_mix_post[0]), pos0=pos0, tt=tt)
        x2 = _ffn(x1.reshape(b * t, d), row(g_ffn_pre[0]), row(g_ffn_post[0]),
                  wg[0], wu[0], wd[0], tm=tm)
        return x2, state

    def layer1_tail(x2, o, tm):
        return _ffn(x2, row(g_ffn_pre[1]), row(g_ffn_post[1]), wg[1], wu[1], wd[1], tm=tm,
                    attn=(o, wo, row(g_mix_post[1])))

    zero_prev = jnp.zeros((batch, POOL_STATE_ROWS, d), F32)
    x2p, state_p = layer0(x_prompt, zero_prev, 0, TOKEN_TILE, TOKEN_TILE)
    kp, vp, _, kpb, vpb, qpb = _proj(x2p, row(g_kv), row(g_mix_pre[1]), wk, wv, wq, tm=TOKEN_TILE)
    op = _prompt_attention(qpb, kpb, vpb, slopes, lq, gsub, batch=batch, seq=seq, lam_init=lam_init)
    y_prompt = layer1_tail(x2p, op, TOKEN_TILE).reshape(batch, seq, d)

    n_s = dec_batch * dec_seq
    x2s, state_s = layer0(x_sample, state_pool[0], past_len, dec_seq, n_s)
    ks, vs, qs, _, _, _ = _proj(x2s, row(g_kv), row(g_mix_pre[1]), wk, wv, wq, tm=n_s)
    os_ = _sample_attention(qs.reshape(dec_batch, dec_seq, -1), ks.reshape(dec_batch, dec_seq, -1),
                            vs.reshape(dec_batch, dec_seq, -1), cache_k, cache_v, page_table,
                            slopes, lq, gsub, lam_init=lam_init)
    y_sample = layer1_tail(x2s, os_, n_s).reshape(dec_batch, dec_seq, d)

    kv_shape_p = (batch, seq, N_HEADS, V_DIM)
    kv_shape_s = (dec_batch, dec_seq, N_HEADS, V_DIM)
    return (y_prompt, y_sample, state_p[None], state_s[None],
            kp.reshape(kv_shape_p), vp.reshape(kv_shape_p),
            ks.reshape(kv_shape_s), vs.reshape(kv_shape_s))
```

```python
import functools
import math

import jax
import jax.numpy as jnp
from jax import lax
from jax.experimental import pallas as pl
from jax.experimental.pallas import tpu as pltpu

F32 = jnp.float32
BF16 = jnp.bfloat16

D_MODEL = 1024
POOL_WINDOWS = (2, 4, 8, 16)
POOL_GROUP_DIM = D_MODEL // len(POOL_WINDOWS)
POOL_STATE_ROWS = max(POOL_WINDOWS) - 1
HALO_ROWS = POOL_STATE_ROWS + 1
HEAD_DIM = 64
V_DIM = 2 * HEAD_DIM
N_HEADS = D_MODEL // V_DIM
PAGE_SIZE = 128
PAGE_ROWS = PAGE_SIZE * N_HEADS
NORM_EPS = 1e-6
SUBLN_EPS = 1e-5
NEG_INF = -1e30
N_A_LAYERS = 1

VMEM_LIMIT_BYTES = 56 * 1024 * 1024
TOKEN_TILE = 512
ATTN_BLOCK = 256
PAGES_PER_STEP = 8
PAGE_BUFFERS = 2
NEW_KEY_PAD = 16


def _rms(xf, g, eps):
    return xf * lax.rsqrt(jnp.mean(xf * xf, axis=-1, keepdims=True) + eps) * g


def _const_spec(shape):
    zeros = (0,) * len(shape)
    return pl.BlockSpec(shape, lambda *_: zeros, pipeline_mode=pl.Buffered(1))


def _log2(n):
    assert n & (n - 1) == 0, n
    return n.bit_length() - 1


def _mixer_body(x_ref, prev_ref, wpool_ref, pscale_ref, gpre_ref, gpost_ref,
                out_ref, state_ref, ext_ref, y_ref, *, tt, n_tiles, pos0):
    j = pl.program_id(1)

    @pl.when(j == 0)
    def _():
        ext_ref[0:1, :] = jnp.zeros((1, D_MODEL), F32)
        ext_ref[1:HALO_ROWS, :] = prev_ref[0]

    x = x_ref[0]
    h = _rms(x, gpre_ref[...], NORM_EPS)
    ext_ref[HALO_ROWS:HALO_ROWS + tt, :] = h
    pos = pos0 + j * tt + lax.broadcasted_iota(jnp.int32, (tt, 1), 0)
    for g, w in enumerate(POOL_WINDOWS):
        cols = slice(g * POOL_GROUP_DIM, (g + 1) * POOL_GROUP_DIM)
        e = h[:, cols]
        s = e
        for k in range(1, w):
            s = s + ext_ref[HALO_ROWS - k:HALO_ROWS - k + tt, cols]
        inv_cnt = 1.0 / jnp.minimum(w, pos + 1).astype(F32)
        diff = (s * inv_cnt - e).astype(BF16)
        yg = jnp.dot(diff, wpool_ref[g], preferred_element_type=F32)
        y_ref[:, cols] = yg * pscale_ref[:, cols]
    out_ref[0] = x + _rms(y_ref[...], gpost_ref[...], NORM_EPS)

    @pl.when(j == n_tiles - 1)
    def _():
        state_ref[0] = ext_ref[tt + 1:tt + HALO_ROWS, :]

    if n_tiles > 1:
        ext_ref[0:HALO_ROWS, :] = ext_ref[tt:tt + HALO_ROWS, :]


def _mixer(x, prev, wpool, pscale, gpre, gpost, *, pos0, tt):
    b, t, d = x.shape
    n_tiles = t // tt
    body = functools.partial(_mixer_body, tt=tt, n_tiles=n_tiles, pos0=pos0)
    return pl.pallas_call(
        body,
        grid=(b, n_tiles),
        in_specs=[
            pl.BlockSpec((1, tt, d), lambda i, j: (i, j, 0)),
            pl.BlockSpec((1, POOL_STATE_ROWS, d), lambda i, j: (i, 0, 0)),
            _const_spec(wpool.shape),
            _const_spec((1, d)),
            _const_spec((1, d)),
            _const_spec((1, d)),
        ],
        out_specs=[
            pl.BlockSpec((1, tt, d), lambda i, j: (i, j, 0)),
            pl.BlockSpec((1, POOL_STATE_ROWS, d), lambda i, j: (i, 0, 0)),
        ],
        out_shape=[
            jax.ShapeDtypeStruct((b, t, d), F32),
            jax.ShapeDtypeStruct((b, POOL_STATE_ROWS, d), F32),
        ],
        scratch_shapes=[
            pltpu.VMEM((HALO_ROWS + tt, d), F32),
            pltpu.VMEM((tt, d), F32),
        ],
        compiler_params=pltpu.CompilerParams(
            dimension_semantics=("arbitrary", "arbitrary"),
            vmem_limit_bytes=VMEM_LIMIT_BYTES),
        name="pool_mixer",
    )(x, prev, wpool, pscale, gpre, gpost)


def _ffn_body(*refs, with_attn, ff_chunks):
    if with_attn:
        (x_ref, o_ref, wo_ref, gmix_ref, gpre_ref, gpost_ref, wg_ref, wu_ref, wd_ref,
         out_ref, act_ref) = refs
    else:
        x_ref, gpre_ref, gpost_ref, wg_ref, wu_ref, wd_ref, out_ref, act_ref = refs
    x = x_ref[...]
    if with_attn:
        y = jnp.dot(o_ref[...].astype(BF16), wo_ref[...], preferred_element_type=F32)
        x = x + _rms(y, gmix_ref[...], NORM_EPS)
    h = _rms(x, gpre_ref[...], NORM_EPS).astype(BF16)
    for c0, c1 in ff_chunks:
        gate = jnp.dot(h, wg_ref[:, c0:c1], preferred_element_type=F32)
        up = jnp.dot(h, wu_ref[:, c0:c1], preferred_element_type=F32)
        act_ref[:, c0:c1] = (gate * jax.nn.sigmoid(gate) * up).astype(BF16)
    f = jnp.dot(act_ref[...], wd_ref[...], preferred_element_type=F32)
    out_ref[...] = x + _rms(f, gpost_ref[...], NORM_EPS)


def _ffn(x, gpre, gpost, wg, wu, wd, *, tm, attn=None):
    n, d = x.shape
    d_ff = wg.shape[1]
    half = d_ff // 2
    ff_chunks = ((0, half), (half, d_ff)) if half % 128 == 0 else ((0, d_ff),)
    row_spec = pl.BlockSpec((tm, d), lambda i: (i, 0))
    args = [x]
    in_specs = [row_spec]
    if attn is not None:
        o, wo, gmix = attn
        args += [o, wo, gmix]
        in_specs += [pl.BlockSpec((tm, o.shape[1]), lambda i: (i, 0)),
                     _const_spec(wo.shape), _const_spec((1, d))]
    args += [gpre, gpost, wg, wu, wd]
    in_specs += [_const_spec((1, d)), _const_spec((1, d)),
                 _const_spec(wg.shape), _const_spec(wu.shape), _const_spec(wd.shape)]
    body = functools.partial(_ffn_body, with_attn=attn is not None, ff_chunks=ff_chunks)
    return pl.pallas_call(
        body,
        grid=(n // tm,),
        in_specs=in_specs,
        out_specs=row_spec,
        out_shape=jax.ShapeDtypeStruct((n, d), F32),
        scratch_shapes=[pltpu.VMEM((tm, d_ff), BF16)],
        compiler_params=pltpu.CompilerParams(
            dimension_semantics=("arbitrary",),
            vmem_limit_bytes=VMEM_LIMIT_BYTES),
        name="attn_out_ffn" if attn is not None else "ffn",
    )(*args)


def _proj_body(x_ref, gkv_ref, gq_ref, wk_ref, wv_ref, wq_ref,
               k_ref, v_ref, q_ref, kb_ref, vtb_ref, qb_ref):
    x = x_ref[...]
    src = _rms(x, gkv_ref[...], NORM_EPS).astype(BF16)
    k = jnp.dot(src, wk_ref[...], preferred_element_type=F32)
    k_ref[...] = k
    kb_ref[...] = k.astype(BF16)
    v = jnp.dot(src, wv_ref[...], preferred_element_type=F32)
    v_ref[...] = v
    vtb_ref[...] = v.T.astype(BF16)
    h = _rms(x, gq_ref[...], NORM_EPS).astype(BF16)
    q = jnp.dot(h, wq_ref[...], preferred_element_type=F32) * (HEAD_DIM ** -0.5)
    q_ref[...] = q
    qb_ref[...] = q.astype(BF16)


def _proj(x, gkv, gq, wk, wv, wq, *, tm):
    n, d = x.shape
    w = wk.shape[1]
    row_in = pl.BlockSpec((tm, d), lambda i: (i, 0))
    row_out = pl.BlockSpec((tm, w), lambda i: (i, 0))
    col_out = pl.BlockSpec((w, tm), lambda i: (0, i))
    row_f32, row_bf16 = jax.ShapeDtypeStruct((n, w), F32), jax.ShapeDtypeStruct((n, w), BF16)
    return pl.pallas_call(
        _proj_body,
        grid=(n // tm,),
        in_specs=[row_in, _const_spec((1, d)), _const_spec((1, d)),
                  _const_spec(wk.shape), _const_spec(wv.shape), _const_spec(wq.shape)],
        out_specs=[row_out, row_out, row_out, row_out, col_out, row_out],
        out_shape=[row_f32, row_f32, row_f32, row_bf16,
                   jax.ShapeDtypeStruct((w, n), BF16), row_bf16],
        compiler_params=pltpu.CompilerParams(
            dimension_semantics=("arbitrary",),
            vmem_limit_bytes=VMEM_LIMIT_BYTES),
        name="kvq_proj",
    )(x, gkv, gq, wk, wv, wq)


def _lambda(lq_ref, lam_init):
    lq = lq_ref[...]
    a = jnp.sum(lq[0:1] * lq[1:2], axis=-1, keepdims=True)
    b = jnp.sum(lq[2:3] * lq[3:4], axis=-1, keepdims=True)
    return jnp.exp(a) - jnp.exp(b) + lam_init


def _diff_finalize(o1, o2, lq_ref, gsub_ref, lam_init):
    od = o1 - _lambda(lq_ref, lam_init) * o2
    return _rms(od, gsub_ref[...], SUBLN_EPS) * (1.0 - lam_init)


def _online_update(m_ref, l_ref, acc_ref, m_cur):
    m_prev = m_ref[...]
    m_new = jnp.maximum(m_prev, m_cur)
    alpha = jnp.exp(m_prev - m_new)
    m_ref[...] = m_new
    l_ref[...] = alpha * l_ref[...]
    acc_ref[...] = alpha * acc_ref[...]
    return m_new


def _prompt_attn_body(slopes_ref, q_ref, k_ref, vt_ref, lq_ref, gsub_ref, o_ref,
                      bias_ref, s_ref, p_ref, *, blk, lam_init):
    seq = q_ref.shape[0]
    nq = seq // blk
    neg_slope = -slopes_ref[pl.program_id(1)]
    contract_last = (((1,), (1,)), ((), ()))

    rel = (lax.broadcasted_iota(jnp.int32, (seq, blk), 1) + (seq - blk)
           - lax.broadcasted_iota(jnp.int32, (seq, blk), 0))
    bias_ref[...] = jnp.where(rel >= 0, neg_slope * rel.astype(F32), NEG_INF)

    lane = lax.broadcasted_iota(jnp.int32, (blk, V_DIM), 1)
    for i in range(nq):
        n_keys = (i + 1) * blk
        q = q_ref[i * blk:(i + 1) * blk, :]
        zero = jnp.zeros_like(q)
        q2 = jnp.concatenate([jnp.where(lane < HEAD_DIM, q, zero),
                              jnp.where(lane >= HEAD_DIM, q, zero)], axis=0)
        s_ref[0:n_keys, :] = lax.dot_general(k_ref[0:n_keys, :], q2, contract_last,
                                             preferred_element_type=F32)
        bias_row0 = (nq - 1 - i) * blk
        m = None
        for j in range(i + 1):
            rows = slice(j * blk, (j + 1) * blk)
            bias = bias_ref[bias_row0 + j * blk:bias_row0 + (j + 1) * blk, :]
            t = s_ref[rows, :] + jnp.concatenate([bias, bias], axis=1)
            s_ref[rows, :] = t
            m_j = jnp.max(t, axis=0, keepdims=True)
            m = m_j if m is None else jnp.maximum(m, m_j)
        l = jnp.zeros_like(m)
        for j in range(i + 1):
            rows = slice(j * blk, (j + 1) * blk)
            p = jnp.exp(s_ref[rows, :] - m)
            l = l + jnp.sum(p, axis=0, keepdims=True)
            p_ref[rows, :] = p.astype(BF16)
        ot = jnp.dot(vt_ref[:, 0:n_keys], p_ref[0:n_keys, :], preferred_element_type=F32) / l
        o = _diff_finalize(ot[:, :blk].T, ot[:, blk:].T, lq_ref, gsub_ref, lam_init)
        o_ref[i * blk:(i + 1) * blk, :] = o.astype(o_ref.dtype)


def _prompt_attention(qb, kb, vtb, slopes, lq, gsub, *, batch, seq, lam_init):
    blk = ATTN_BLOCK
    body = functools.partial(_prompt_attn_body, blk=blk, lam_init=lam_init)
    head_rows = pl.BlockSpec((seq, V_DIM), lambda b, h, s: (b, h))
    grid_spec = pltpu.PrefetchScalarGridSpec(
        num_scalar_prefetch=1,
        grid=(batch, N_HEADS),
        in_specs=[
            head_rows,
            head_rows,
            pl.BlockSpec((V_DIM, seq), lambda b, h, s: (h, b)),
            pl.BlockSpec(lq.shape, lambda b, h, s: (0, 0)),
            pl.BlockSpec(gsub.shape, lambda b, h, s: (0, 0)),
        ],
        out_specs=head_rows,
        scratch_shapes=[
            pltpu.VMEM((seq, blk), F32),
            pltpu.VMEM((seq, 2 * blk), F32),
            pltpu.VMEM((seq, 2 * blk), BF16),
        ],
    )
    return pl.pallas_call(
        body,
        grid_spec=grid_spec,
        out_shape=jax.ShapeDtypeStruct(qb.shape, BF16),
        compiler_params=pltpu.CompilerParams(
            dimension_semantics=("arbitrary", "arbitrary"),
            vmem_limit_bytes=VMEM_LIMIT_BYTES),
        name="prompt_diff_attention",
    )(slopes, qb, kb, vtb, lq, gsub)


def _sample_attn_body(pt_ref, qall_ref, slope_ref, *rest, pages_per_step, n_steps, past_len,
                      dec_seq, lam_init):
    k_refs = rest[:pages_per_step]
    v_refs = rest[pages_per_step:2 * pages_per_step]
    (knew_ref, vnew_ref, lq_ref, gsub_ref, o_ref,
     bias_ref, m_ref, l_ref, acc_ref) = rest[2 * pages_per_step:]
    del pt_ref
    step = pl.program_id(1)
    n_rows = qall_ref.shape[1]
    slope = slope_ref[:, 0:1]
    contract_last = (((1,), (1,)), ((), ()))

    def row_col_ids(shape):
        r = lax.broadcasted_iota(jnp.int32, shape, 0)
        c = lax.broadcasted_iota(jnp.int32, shape, 1)
        head_q = (r >> _log2(dec_seq)) & (N_HEADS - 1)
        return head_q, r & (dec_seq - 1), c >> _log2(N_HEADS), c & (N_HEADS - 1)

    @pl.when(step == 0)
    def _():
        hq, tok, key, hk = row_col_ids((n_rows, PAGE_ROWS))
        bias_ref[...] = jnp.where(hq == hk, -slope * (tok - key).astype(F32), NEG_INF)
        m_ref[...] = jnp.full(m_ref.shape, NEG_INF, F32)
        l_ref[...] = jnp.zeros(l_ref.shape, F32)
        acc_ref[...] = jnp.zeros(acc_ref.shape, F32)

    q = qall_ref[0]
    bias = bias_ref[...]
    parts = []
    m_cur = None
    for i in range(pages_per_step):
        s = lax.dot_general(q, k_refs[i][...], contract_last, preferred_element_type=F32)
        t = s + bias
        m_page = jnp.max(t, axis=-1, keepdims=True)
        p = jnp.exp(t - m_page)
        l_page = jnp.sum(p, axis=-1, keepdims=True)
        acc_page = jnp.dot(p, v_refs[i][...], preferred_element_type=F32)
        page_pos = (step * pages_per_step + i) * PAGE_SIZE
        m_page = m_page - slope * (past_len - page_pos).astype(F32)
        m_cur = m_page if m_cur is None else jnp.maximum(m_cur, m_page)
        parts.append((m_page, l_page, acc_page))
    m_new = _online_update(m_ref, l_ref, acc_ref, m_cur)
    l_add = jnp.zeros(l_ref.shape, F32)
    acc_add = jnp.zeros(acc_ref.shape, F32)
    for m_page, l_page, acc_page in parts:
        w = jnp.exp(m_page - m_new)
        l_add = l_add + w * l_page
        acc_add = acc_add + w * acc_page
    l_ref[...] += l_add
    acc_ref[...] += acc_add

    @pl.when(step == n_steps - 1)
    def _():
        hq, tok, key, hk = row_col_ids((n_rows, knew_ref.shape[1]))
        valid = (hq == hk) & (key <= tok)
        bias_new = jnp.where(valid, -slope * (tok - key).astype(F32), NEG_INF)
        s = lax.dot_general(q, knew_ref[0], contract_last, preferred_element_type=F32)
        t = s + bias_new
        m_fin = _online_update(m_ref, l_ref, acc_ref, jnp.max(t, axis=-1, keepdims=True))
        p = jnp.exp(t - m_fin)
        l = l_ref[...] + jnp.sum(p, axis=-1, keepdims=True)
        acc = acc_ref[...] + jnp.dot(p, vnew_ref[0], preferred_element_type=F32)
        o = acc / l
        half = n_rows // 2
        o_ref[0] = _diff_finalize(o[:half], o[half:], lq_ref, gsub_ref, lam_init)


def _page_index(b, s, pt, *, i, n_pages, pages_per_step):
    return (pt[b * n_pages + s * pages_per_step + i], 0)


def _sample_attention(q_s, k_s, v_s, cache_k, cache_v, page_table, slopes, lq, gsub, *, lam_init):
    dec_batch, dec_seq, _ = q_s.shape
    n_pages = page_table.shape[1]
    pps = PAGES_PER_STEP
    n_steps = n_pages // pps
    n_rows = 2 * N_HEADS * dec_seq

    q5 = q_s.reshape(dec_batch, dec_seq, N_HEADS, 2, HEAD_DIM)
    zeros = jnp.zeros_like(q5[..., 0, :])
    qm = jnp.stack([jnp.concatenate([q5[..., 0, :], zeros], axis=-1),
                    jnp.concatenate([zeros, q5[..., 1, :]], axis=-1)], axis=1)
    qall = qm.transpose(0, 1, 3, 2, 4).reshape(dec_batch, n_rows, V_DIM)
    row_slope = jnp.broadcast_to(jnp.tile(jnp.repeat(slopes, dec_seq), 2)[:, None], (n_rows, 128))

    def pad_new(a):
        a = a.reshape(dec_batch, dec_seq, N_HEADS, V_DIM)
        a = jnp.pad(a, ((0, 0), (0, NEW_KEY_PAD - dec_seq), (0, 0), (0, 0)))
        return a.reshape(dec_batch, NEW_KEY_PAD * N_HEADS, V_DIM)

    knew, vnew = pad_new(k_s), pad_new(v_s)
    ck = cache_k.reshape(-1, V_DIM)
    cv = cache_v.reshape(-1, V_DIM)
    page_specs = [
        pl.BlockSpec((PAGE_ROWS, V_DIM),
                     functools.partial(_page_index, i=i, n_pages=n_pages, pages_per_step=pps),
                     pipeline_mode=pl.Buffered(PAGE_BUFFERS))
        for i in range(pps)]
    body = functools.partial(_sample_attn_body, pages_per_step=pps, n_steps=n_steps,
                             past_len=n_pages * PAGE_SIZE, dec_seq=dec_seq, lam_init=lam_init)
    grid_spec = pltpu.PrefetchScalarGridSpec(
        num_scalar_prefetch=1,
        grid=(dec_batch, n_steps),
        in_specs=[
            pl.BlockSpec((1, n_rows, V_DIM), lambda b, s, pt: (b, 0, 0)),
            pl.BlockSpec(row_slope.shape, lambda b, s, pt: (0, 0)),
            *page_specs, *page_specs,
            pl.BlockSpec((1,) + knew.shape[1:], lambda b, s, pt: (b, 0, 0)),
            pl.BlockSpec((1,) + vnew.shape[1:], lambda b, s, pt: (b, 0, 0)),
            pl.BlockSpec(lq.shape, lambda b, s, pt: (0, 0)),
            pl.BlockSpec(gsub.shape, lambda b, s, pt: (0, 0)),
        ],
        out_specs=pl.BlockSpec((1, n_rows // 2, V_DIM), lambda b, s, pt: (b, 0, 0)),
        scratch_shapes=[
            pltpu.VMEM((n_rows, PAGE_ROWS), F32),
            pltpu.VMEM((n_rows, 1), F32),
            pltpu.VMEM((n_rows, 1), F32),
            pltpu.VMEM((n_rows, V_DIM), F32),
        ],
    )
    o = pl.pallas_call(
        body,
        grid_spec=grid_spec,
        out_shape=jax.ShapeDtypeStruct((dec_batch, n_rows // 2, V_DIM), F32),
        compiler_params=pltpu.CompilerParams(
            dimension_semantics=("arbitrary", "arbitrary"),
            vmem_limit_bytes=VMEM_LIMIT_BYTES),
        name="sample_paged_diff_attention",
    )(page_table.reshape(-1), qall, row_slope, *([ck] * pps), *([cv] * pps), knew, vnew, lq, gsub)
    o = o.reshape(dec_batch, N_HEADS, dec_seq, V_DIM).transpose(0, 2, 1, 3)
    return o.reshape(dec_batch * dec_seq, N_HEADS * V_DIM)


def kernel(x_prompt, x_sample, state_pool, cache_k, cache_v, page_table, w_pool, pool_scale, g_kv,
           w_k, w_v, w_q, w_o, lambda_qk, g_sub, g_mix_pre, g_mix_post, g_ffn_pre, g_ffn_post,
           w_gate, w_up, w_down):
    batch, seq, d = x_prompt.shape
    dec_batch, dec_seq, _ = x_sample.shape
    past_len = page_table.shape[1] * PAGE_SIZE
    depth = g_mix_pre.shape[0]
    assert depth == 2 and w_pool.shape[0] == N_A_LAYERS and d == D_MODEL

    row = lambda a: a.reshape(1, -1)
    wpool = w_pool[0].astype(BF16)
    wg, wu, wd = (w.astype(BF16) for w in (w_gate, w_up, w_down))
    wk, wv, wq, wo = (w.astype(BF16) for w in (w_k, w_v, w_q[0], w_o[0]))
    slopes = jnp.exp2(-8.0 * (jnp.arange(N_HEADS, dtype=F32) + 1.0) / N_HEADS)
    lam_init = 0.8 - 0.6 * math.exp(-0.3 * 1)
    lq = lambda_qk[0].astype(F32)
    gsub = row(g_sub[0])

    def layer0(x, prev, pos0, tt, tm):
        b, t, _ = x.shape
        x1, state = _mixer(x, prev, wpool, row(pool_scale[0]), row(g_mix_pre[0]),
                           row(g_mix_post[0]), pos0=pos0, tt=tt)
        x2 = _ffn(x1.reshape(b * t, d), row(g_ffn_pre[0]), row(g_ffn_post[0]),
                  wg[0], wu[0], wd[0], tm=tm)
        return x2, state

    def layer1_tail(x2, o, tm):
        return _ffn(x2, row(g_ffn_pre[1]), row(g_ffn_post[1]), wg[1], wu[1], wd[1], tm=tm,
                    attn=(o, wo, row(g_mix_post[1])))

    zero_prev = jnp.zeros((batch, POOL_STATE_ROWS, d), F32)
    x2p, state_p = layer0(x_prompt, zero_prev, 0, TOKEN_TILE, TOKEN_TILE)
    kp, vp, _, kpb, vtpb, qpb = _proj(x2p, row(g_kv), row(g_mix_pre[1]), wk, wv, wq, tm=TOKEN_TILE)
    op = _prompt_attention(qpb, kpb, vtpb, slopes, lq, gsub, batch=batch, seq=seq, lam_init=lam_init)
    y_prompt = layer1_tail(x2p, op, TOKEN_TILE).reshape(batch, seq, d)

    n_s = dec_batch * dec_seq
    x2s, state_s = layer0(x_sample, state_pool[0], past_len, dec_seq, n_s)
    ks, vs, qs, _, _, _ = _proj(x2s, row(g_kv), row(g_mix_pre[1]), wk, wv, wq, tm=n_s)
    os_ = _sample_attention(qs.reshape(dec_batch, dec_seq, -1), ks.reshape(dec_batch, dec_seq, -1),
                            vs.reshape(dec_batch, dec_seq, -1), cache_k, cache_v, page_table,
                            slopes, lq, gsub, lam_init=lam_init)
    y_sample = layer1_tail(x2s, os_, n_s).reshape(dec_batch, dec_seq, d)

    kv_shape_p = (batch, seq, N_HEADS, V_DIM)
    kv_shape_s = (dec_batch, dec_seq, N_HEADS, V_DIM)
    return (y_prompt, y_sample, state_p[None], state_s[None],
            kp.reshape(kv_shape_p), vp.reshape(kv_shape_p),
            ks.reshape(kv_shape_s), vs.reshape(kv_shape_s))
```

```python
import functools
import math

import jax
import jax.numpy as jnp
from jax import lax
from jax.experimental import pallas as pl
from jax.experimental.pallas import tpu as pltpu

F32 = jnp.float32
BF16 = jnp.bfloat16

D_MODEL = 1024
POOL_WINDOWS = (2, 4, 8, 16)
POOL_GROUP_DIM = D_MODEL // len(POOL_WINDOWS)
POOL_STATE_ROWS = max(POOL_WINDOWS) - 1
HALO_ROWS = POOL_STATE_ROWS + 1
HEAD_DIM = 64
V_DIM = 2 * HEAD_DIM
N_HEADS = D_MODEL // V_DIM
PAGE_SIZE = 128
PAGE_ROWS = PAGE_SIZE * N_HEADS
NORM_EPS = 1e-6
SUBLN_EPS = 1e-5
NEG_INF = -1e30
N_A_LAYERS = 1

VMEM_LIMIT_BYTES = 56 * 1024 * 1024
TOKEN_TILE = 512
ATTN_BLOCK = 256
PAGES_PER_STEP = 8
NEW_KEY_PAD = 16
LOG2E = math.log2(math.e)
POS_RADIX = 128
N_POS_FEATURES = 12


def _rms(xf, g, eps):
    return xf * lax.rsqrt(jnp.mean(xf * xf, axis=-1, keepdims=True) + eps) * g


def _const_spec(shape):
    zeros = (0,) * len(shape)
    return pl.BlockSpec(shape, lambda *_: zeros, pipeline_mode=pl.Buffered(1))


def _log2(n):
    assert n & (n - 1) == 0, n
    return n.bit_length() - 1


def _mixer_body(x_ref, prev_ref, wpool_ref, pscale_ref, gpre_ref, gpost_ref,
                out_ref, state_ref, ext_ref, y_ref, *, tt, n_tiles, pos0):
    j = pl.program_id(1)

    @pl.when(j == 0)
    def _():
        ext_ref[0:1, :] = jnp.zeros((1, D_MODEL), F32)
        ext_ref[1:HALO_ROWS, :] = prev_ref[0]

    x = x_ref[0]
    h = _rms(x, gpre_ref[...], NORM_EPS)
    ext_ref[HALO_ROWS:HALO_ROWS + tt, :] = h
    pos = pos0 + j * tt + lax.broadcasted_iota(jnp.int32, (tt, 1), 0)
    for g, w in enumerate(POOL_WINDOWS):
        cols = slice(g * POOL_GROUP_DIM, (g + 1) * POOL_GROUP_DIM)
        e = h[:, cols]
        s = e
        for k in range(1, w):
            s = s + ext_ref[HALO_ROWS - k:HALO_ROWS - k + tt, cols]
        inv_cnt = 1.0 / jnp.minimum(w, pos + 1).astype(F32)
        diff = (s * inv_cnt - e).astype(BF16)
        yg = jnp.dot(diff, wpool_ref[g], preferred_element_type=F32)
        y_ref[:, cols] = yg * pscale_ref[:, cols]
    out_ref[0] = x + _rms(y_ref[...], gpost_ref[...], NORM_EPS)

    @pl.when(j == n_tiles - 1)
    def _():
        state_ref[0] = ext_ref[tt + 1:tt + HALO_ROWS, :]

    if n_tiles > 1:
        ext_ref[0:HALO_ROWS, :] = ext_ref[tt:tt + HALO_ROWS, :]


def _mixer(x, prev, wpool, pscale, gpre, gpost, *, pos0, tt):
    b, t, d = x.shape
    n_tiles = t // tt
    body = functools.partial(_mixer_body, tt=tt, n_tiles=n_tiles, pos0=pos0)
    return pl.pallas_call(
        body,
        grid=(b, n_tiles),
        in_specs=[
            pl.BlockSpec((1, tt, d), lambda i, j: (i, j, 0)),
            pl.BlockSpec((1, POOL_STATE_ROWS, d), lambda i, j: (i, 0, 0)),
            _const_spec(wpool.shape),
            _const_spec((1, d)),
            _const_spec((1, d)),
            _const_spec((1, d)),
        ],
        out_specs=[
            pl.BlockSpec((1, tt, d), lambda i, j: (i, j, 0)),
            pl.BlockSpec((1, POOL_STATE_ROWS, d), lambda i, j: (i, 0, 0)),
        ],
        out_shape=[
            jax.ShapeDtypeStruct((b, t, d), F32),
            jax.ShapeDtypeStruct((b, POOL_STATE_ROWS, d), F32),
        ],
        scratch_shapes=[
            pltpu.VMEM((HALO_ROWS + tt, d), F32),
            pltpu.VMEM((tt, d), F32),
        ],
        compiler_params=pltpu.CompilerParams(
            dimension_semantics=("arbitrary", "arbitrary"),
            vmem_limit_bytes=VMEM_LIMIT_BYTES),
        name="pool_mixer",
    )(x, prev, wpool, pscale, gpre, gpost)


def _ffn_body(*refs, with_attn, ff_chunks):
    if with_attn:
        (x_ref, o_ref, wo_ref, gmix_ref, gpre_ref, gpost_ref, wg_ref, wu_ref, wd_ref,
         out_ref, act_ref) = refs
    else:
        x_ref, gpre_ref, gpost_ref, wg_ref, wu_ref, wd_ref, out_ref, act_ref = refs
    x = x_ref[...]
    if with_attn:
        y = jnp.dot(o_ref[...].astype(BF16), wo_ref[...], preferred_element_type=F32)
        x = x + _rms(y, gmix_ref[...], NORM_EPS)
    h = _rms(x, gpre_ref[...], NORM_EPS).astype(BF16)
    for c0, c1 in ff_chunks:
        gate = jnp.dot(h, wg_ref[:, c0:c1], preferred_element_type=F32)
        up = jnp.dot(h, wu_ref[:, c0:c1], preferred_element_type=F32)
        act_ref[:, c0:c1] = (gate * jax.nn.sigmoid(gate) * up).astype(BF16)
    f = jnp.dot(act_ref[...], wd_ref[...], preferred_element_type=F32)
    out_ref[...] = x + _rms(f, gpost_ref[...], NORM_EPS)


def _ffn(x, gpre, gpost, wg, wu, wd, *, tm, attn=None):
    n, d = x.shape
    d_ff = wg.shape[1]
    half = d_ff // 2
    ff_chunks = ((0, half), (half, d_ff)) if half % 128 == 0 else ((0, d_ff),)
    row_spec = pl.BlockSpec((tm, d), lambda i: (i, 0))
    args = [x]
    in_specs = [row_spec]
    if attn is not None:
        o, wo, gmix = attn
        args += [o, wo, gmix]
        in_specs += [pl.BlockSpec((tm, o.shape[1]), lambda i: (i, 0)),
                     _const_spec(wo.shape), _const_spec((1, d))]
    args += [gpre, gpost, wg, wu, wd]
    in_specs += [_const_spec((1, d)), _const_spec((1, d)),
                 _const_spec(wg.shape), _const_spec(wu.shape), _const_spec(wd.shape)]
    body = functools.partial(_ffn_body, with_attn=attn is not None, ff_chunks=ff_chunks)
    return pl.pallas_call(
        body,
        grid=(n // tm,),
        in_specs=in_specs,
        out_specs=row_spec,
        out_shape=jax.ShapeDtypeStruct((n, d), F32),
        scratch_shapes=[pltpu.VMEM((tm, d_ff), BF16)],
        compiler_params=pltpu.CompilerParams(
            dimension_semantics=("arbitrary",),
            vmem_limit_bytes=VMEM_LIMIT_BYTES),
        name="attn_out_ffn" if attn is not None else "ffn",
    )(*args)


def _proj_body(x_ref, gkv_ref, gq_ref, wk_ref, wv_ref, wq_ref,
               k_ref, v_ref, q_ref, kb_ref, vtb_ref, qb_ref):
    x = x_ref[...]
    src = _rms(x, gkv_ref[...], NORM_EPS).astype(BF16)
    k = jnp.dot(src, wk_ref[...], preferred_element_type=F32)
    k_ref[...] = k
    kb_ref[...] = k.astype(BF16)
    v = jnp.dot(src, wv_ref[...], preferred_element_type=F32)
    v_ref[...] = v
    vtb_ref[...] = v.T.astype(BF16)
    h = _rms(x, gq_ref[...], NORM_EPS).astype(BF16)
    q = jnp.dot(h, wq_ref[...], preferred_element_type=F32) * (HEAD_DIM ** -0.5)
    q_ref[...] = q
    qb_ref[...] = (q * LOG2E).astype(BF16)


def _proj(x, gkv, gq, wk, wv, wq, *, tm):
    n, d = x.shape
    w = wk.shape[1]
    row_in = pl.BlockSpec((tm, d), lambda i: (i, 0))
    row_out = pl.BlockSpec((tm, w), lambda i: (i, 0))
    col_out = pl.BlockSpec((w, tm), lambda i: (0, i))
    row_f32, row_bf16 = jax.ShapeDtypeStruct((n, w), F32), jax.ShapeDtypeStruct((n, w), BF16)
    return pl.pallas_call(
        _proj_body,
        grid=(n // tm,),
        in_specs=[row_in, _const_spec((1, d)), _const_spec((1, d)),
                  _const_spec(wk.shape), _const_spec(wv.shape), _const_spec(wq.shape)],
        out_specs=[row_out, row_out, row_out, row_out, col_out, row_out],
        out_shape=[row_f32, row_f32, row_f32, row_bf16,
                   jax.ShapeDtypeStruct((w, n), BF16), row_bf16],
        compiler_params=pltpu.CompilerParams(
            dimension_semantics=("arbitrary",),
            vmem_limit_bytes=VMEM_LIMIT_BYTES),
        name="kvq_proj",
    )(x, gkv, gq, wk, wv, wq)


def _lambda(lq_ref, lam_init):
    lq = lq_ref[...]
    a = jnp.sum(lq[0:1] * lq[1:2], axis=-1, keepdims=True)
    b = jnp.sum(lq[2:3] * lq[3:4], axis=-1, keepdims=True)
    return jnp.exp(a) - jnp.exp(b) + lam_init


def _diff_finalize(o1, o2, lq_ref, gsub_ref, lam_init):
    od = o1 - _lambda(lq_ref, lam_init) * o2
    return _rms(od, gsub_ref[...], SUBLN_EPS) * (1.0 - lam_init)


def _online_update(m_ref, l_ref, acc_ref, m_cur):
    m_prev = m_ref[...]
    m_new = jnp.maximum(m_prev, m_cur)
    alpha = jnp.exp(m_prev - m_new)
    m_ref[...] = m_new
    l_ref[...] = alpha * l_ref[...]
    acc_ref[...] = alpha * acc_ref[...]
    return m_new


def _alibi_feature_rows(slopes):
    cs = slopes * LOG2E
    c1 = cs.astype(BF16).astype(F32)
    c2 = (cs - c1).astype(BF16).astype(F32)
    c3 = (cs - c1 - c2).astype(BF16).astype(F32)
    c = jnp.stack([c1, c2, c3], axis=1)
    half = N_POS_FEATURES // 2
    pad = jnp.zeros((slopes.shape[0], V_DIM - N_POS_FEATURES), F32)
    zeros = jnp.zeros((slopes.shape[0], half), F32)
    q_rows = jnp.concatenate([POS_RADIX * c, c, zeros, pad], axis=1)
    k_rows = jnp.concatenate([zeros, -POS_RADIX * c, -c, pad], axis=1)
    return q_rows, k_rows


def _prompt_phases(item, q_ref, k_ref, vt_ref, qrow_ref, krow_ref, lq_ref, gsub_ref, o_ref,
                   pos_ref, mask_ref, kaug_ref, qfeat_ref, s_ref, p_ref, *, blk, lam_init):
    seq = q_ref.shape[0]
    nq = seq // blk
    q_block = item % nq
    head = (item // nq) % N_HEADS
    half = N_POS_FEATURES // 2
    contract_last = (((1,), (1,)), ((), ()))

    def init_tables():
        pos = lax.broadcasted_iota(jnp.int32, (seq, V_DIM), 0)
        lane = lax.broadcasted_iota(jnp.int32, (seq, V_DIM), 1)
        hi = (pos >> _log2(POS_RADIX)).astype(F32)
        lo = (pos & (POS_RADIX - 1)).astype(F32)
        third = half // 2
        is_hi = (lane < third) | ((lane >= half) & (lane < half + third))
        pos_ref[...] = jnp.where(lane >= N_POS_FEATURES, 0.0, jnp.where(is_hi, hi, lo))
        key = lax.broadcasted_iota(jnp.int32, (blk, 2 * blk), 0)
        qry = lax.broadcasted_iota(jnp.int32, (blk, 2 * blk), 1) & (blk - 1)
        mask_ref[...] = jnp.where(key <= qry, 0.0, NEG_INF)

    def prepare_head():
        lane = lax.broadcasted_iota(jnp.int32, (seq, V_DIM), 1)
        pos_feat = pos_ref[...]
        kaug_ref[:, 0:V_DIM] = k_ref[...]
        kaug_ref[:, V_DIM:] = jnp.where(lane < half, pos_feat,
                                        krow_ref[pl.ds(head, 1), :]).astype(BF16)
        qfeat_ref[...] = jnp.where(lane < half, qrow_ref[pl.ds(head, 1), :], pos_feat).astype(BF16)

    def query_block(i):
        lane = lax.broadcasted_iota(jnp.int32, (blk, V_DIM), 1)
        n_keys = (i + 1) * blk
        q = q_ref[i * blk:(i + 1) * blk, :]
        zero = jnp.zeros_like(q)
        qf = qfeat_ref[i * blk:(i + 1) * blk, :]
        q2 = jnp.concatenate(
            [jnp.concatenate([jnp.where(lane < HEAD_DIM, q, zero), qf], axis=1),
             jnp.concatenate([jnp.where(lane >= HEAD_DIM, q, zero), qf], axis=1)], axis=0)
        s_ref[0:n_keys, :] = lax.dot_general(kaug_ref[0:n_keys, :], q2, contract_last,
                                             preferred_element_type=F32)
        diag = slice(i * blk, (i + 1) * blk)
        s_ref[diag, :] = s_ref[diag, :] + mask_ref[...]
        m = None
        for j in range(i + 1):
            m_j = jnp.max(s_ref[j * blk:(j + 1) * blk, :], axis=0, keepdims=True)
            m = m_j if m is None else jnp.maximum(m, m_j)
        l = jnp.zeros_like(m)
        for j in range(i + 1):
            rows = slice(j * blk, (j + 1) * blk)
            p = jnp.exp2(s_ref[rows, :] - m)
            l = l + jnp.sum(p, axis=0, keepdims=True)
            p_ref[rows, :] = p.astype(BF16)
        ot = jnp.dot(vt_ref[:, 0:n_keys], p_ref[0:n_keys, :], preferred_element_type=F32) / l
        o = _diff_finalize(ot[:, :blk].T, ot[:, blk:].T, lq_ref, gsub_ref, lam_init)
        o_ref[i * blk:(i + 1) * blk, :] = o.astype(o_ref.dtype)

    def prepare():
        pl.when(item == 0)(init_tables)
        pl.when(q_block == 0)(prepare_head)

    return prepare, q_block, [functools.partial(query_block, i) for i in range(nq)]


def _prompt_scratch(seq, blk):
    return [
        pltpu.VMEM((seq, V_DIM), F32),
        pltpu.VMEM((blk, 2 * blk), F32),
        pltpu.VMEM((seq, 2 * V_DIM), BF16),
        pltpu.VMEM((seq, V_DIM), BF16),
        pltpu.VMEM((seq, 2 * blk), F32),
        pltpu.VMEM((seq, 2 * blk), BF16),
    ]


def _sample_phases(g, q_ref, qprev_ref, slope_ref, k_refs, v_refs, knew_ref, vnew_ref, lq_ref,
                   gsub_ref, o_ref, bias_ref, p_ref, mp_ref, lp_ref, m_ref, l_ref, acc_ref, *,
                   n_steps, n_groups, past_len, dec_seq, lam_init):
    pages_per_step = len(k_refs)
    prev_step = _value_group(g) % n_steps
    n_rows = q_ref.shape[1]
    slope = slope_ref[:, 0:1]
    contract_last = (((1,), (1,)), ((), ()))

    def row_col_ids(shape):
        r = lax.broadcasted_iota(jnp.int32, shape, 0)
        c = lax.broadcasted_iota(jnp.int32, shape, 1)
        head_q = (r >> _log2(dec_seq)) & (N_HEADS - 1)
        return head_q, r & (dec_seq - 1), c >> _log2(N_HEADS), c & (N_HEADS - 1)

    def init_tables():
        hq, tok, key, hk = row_col_ids((n_rows, PAGE_ROWS))
        bias_ref[...] = jnp.where(hq == hk, -slope * (tok - key).astype(F32), NEG_INF)
        p_ref[...] = jnp.zeros(p_ref.shape, p_ref.dtype)
        mp_ref[...] = jnp.full(mp_ref.shape, NEG_INF, F32)
        lp_ref[...] = jnp.zeros(lp_ref.shape, F32)

    def reset_row():
        m_ref[...] = jnp.full(m_ref.shape, NEG_INF, F32)
        l_ref[...] = jnp.zeros(l_ref.shape, F32)
        acc_ref[...] = jnp.zeros(acc_ref.shape, F32)

    def prepare():
        pl.when(g == 0)(init_tables)
        pl.when(prev_step == 0)(reset_row)

    def body():
        m_cur = mp_ref[0]
        for i in range(1, pages_per_step):
            m_cur = jnp.maximum(m_cur, mp_ref[i])
        m_new = _online_update(m_ref, l_ref, acc_ref, m_cur)
        l_add = jnp.zeros(l_ref.shape, F32)
        acc_add = jnp.zeros(acc_ref.shape, F32)
        for i in range(pages_per_step):
            w = jnp.exp(mp_ref[i] - m_new)
            l_add = l_add + w * lp_ref[i]
            acc_add = acc_add + w * jnp.dot(p_ref[i], v_refs[i][...], preferred_element_type=F32)
        l_ref[...] += l_add
        acc_ref[...] += acc_add

        q = q_ref[0]
        bias = bias_ref[...]
        step = _score_group(g, n_groups) % n_steps
        for i in range(pages_per_step):
            s = lax.dot_general(q, k_refs[i][...], contract_last, preferred_element_type=F32)
            t = s + bias
            m_page = jnp.max(t, axis=-1, keepdims=True)
            p = jnp.exp(t - m_page)
            p_ref[i] = p.astype(p_ref.dtype)
            lp_ref[i] = jnp.sum(p, axis=-1, keepdims=True)
            page_pos = (step * pages_per_step + i) * PAGE_SIZE
            mp_ref[i] = m_page - slope * (past_len - page_pos).astype(F32)

    def finish_row():
        qp = qprev_ref[0]
        hq, tok, key, hk = row_col_ids((n_rows, knew_ref.shape[1]))
        valid = (hq == hk) & (key <= tok)
        bias_new = jnp.where(valid, -slope * (tok - key).astype(F32), NEG_INF)
        s = lax.dot_general(qp, knew_ref[0], contract_last, preferred_element_type=F32)
        t = s + bias_new
        m_fin = _online_update(m_ref, l_ref, acc_ref, jnp.max(t, axis=-1, keepdims=True))
        p = jnp.exp(t - m_fin)
        l = l_ref[...] + jnp.sum(p, axis=-1, keepdims=True)
        acc = acc_ref[...] + jnp.dot(p, vnew_ref[0], preferred_element_type=F32)
        o = acc / l
        half = n_rows // 2
        o_ref[0] = _diff_finalize(o[:half], o[half:], lq_ref, gsub_ref, lam_init)

    def finish():
        pl.when((g > 0) & (prev_step == n_steps - 1))(finish_row)

    return prepare, body, finish


def _score_group(g, n_groups):
    return jnp.minimum(g, n_groups - 1)


def _value_group(g):
    return jnp.maximum(g - 1, 0)


def _key_page_index(g, pt, *, i, n_groups, pages_per_step):
    return (pt[_score_group(g, n_groups) * pages_per_step + i], 0)


def _value_page_index(g, pt, *, i, pages_per_step):
    return (pt[_value_group(g) * pages_per_step + i], 0)


def _attention_body(pt_ref, *refs, pages_per_step, n_items, blk, sample_args, lam_init):
    del pt_ref
    pps = pages_per_step
    q_ref, qprev_ref, slope_ref = refs[:3]
    k_refs = refs[3:3 + pps]
    v_refs = refs[3 + pps:3 + 2 * pps]
    (knew_ref, vnew_ref, lq_ref, gsub_ref, qb_ref, kb_ref, vtb_ref, qrow_ref, krow_ref,
     os_ref, op_ref) = refs[3 + 2 * pps:14 + 2 * pps]
    sample_scratch = refs[14 + 2 * pps:21 + 2 * pps]
    prompt_scratch = refs[21 + 2 * pps:]
    g = pl.program_id(0)
    sample_prepare, sample_body, sample_finish = _sample_phases(
        g, q_ref, qprev_ref, slope_ref, k_refs, v_refs, knew_ref, vnew_ref, lq_ref, gsub_ref,
        os_ref, *sample_scratch, lam_init=lam_init, **sample_args)
    has_item = g < n_items
    prompt_prepare, q_block, query_blocks = _prompt_phases(
        jnp.minimum(g, n_items - 1), qb_ref, kb_ref, vtb_ref, qrow_ref, krow_ref, lq_ref,
        gsub_ref, op_ref, *prompt_scratch, blk=blk, lam_init=lam_init)

    sample_prepare()
    pl.when(has_item)(prompt_prepare)

    def both(query_block):
        sample_body()
        query_block()

    for i, query_block in enumerate(query_blocks):
        pl.when(has_item & (q_block == i))(functools.partial(both, query_block))
    pl.when(jnp.logical_not(has_item))(sample_body)
    sample_finish()


def _attention(qb, kb, vtb, q_s, k_s, v_s, cache_k, cache_v, page_table, slopes, lq, gsub, *,
               batch, seq, lam_init):
    dec_batch, dec_seq, _ = q_s.shape
    n_pages = page_table.shape[1]
    pps = PAGES_PER_STEP
    n_steps = n_pages // pps
    n_groups = dec_batch * n_steps
    n_rows = 2 * N_HEADS * dec_seq
    blk = ATTN_BLOCK
    nq = seq // blk
    n_items = batch * N_HEADS * nq
    assert n_items <= n_groups + 1, (n_items, n_groups)

    q5 = q_s.reshape(dec_batch, dec_seq, N_HEADS, 2, HEAD_DIM)
    zeros = jnp.zeros_like(q5[..., 0, :])
    qm = jnp.stack([jnp.concatenate([q5[..., 0, :], zeros], axis=-1),
                    jnp.concatenate([zeros, q5[..., 1, :]], axis=-1)], axis=1)
    qall = qm.transpose(0, 1, 3, 2, 4).reshape(dec_batch, n_rows, V_DIM)
    row_slope = jnp.broadcast_to(jnp.tile(jnp.repeat(slopes, dec_seq), 2)[:, None], (n_rows, 128))

    def pad_new(a):
        a = a.reshape(dec_batch, dec_seq, N_HEADS, V_DIM)
        a = jnp.pad(a, ((0, 0), (0, NEW_KEY_PAD - dec_seq), (0, 0), (0, 0)))
        return a.reshape(dec_batch, NEW_KEY_PAD * N_HEADS, V_DIM)

    knew, vnew = pad_new(k_s), pad_new(v_s)
    ck = cache_k.reshape(-1, V_DIM)
    cv = cache_v.reshape(-1, V_DIM)
    q_rows, k_rows = _alibi_feature_rows(slopes)

    key_specs = [
        pl.BlockSpec((PAGE_ROWS, V_DIM), functools.partial(
            _key_page_index, i=i, n_groups=n_groups, pages_per_step=pps))
        for i in range(pps)]
    value_specs = [
        pl.BlockSpec((PAGE_ROWS, V_DIM), functools.partial(
            _value_page_index, i=i, pages_per_step=pps))
        for i in range(pps)]
    score_row = lambda g, pt: (_score_group(g, n_groups) // n_steps, 0, 0)
    value_row = lambda g, pt: (_value_group(g) // n_steps, 0, 0)
    const2 = lambda g, pt: (0, 0)

    def pair(g):
        p = jnp.minimum(g, n_items - 1) // nq
        return p // N_HEADS, p % N_HEADS

    head_rows = pl.BlockSpec((seq, V_DIM), lambda g, pt: pair(g))
    head_cols = pl.BlockSpec((V_DIM, seq), lambda g, pt: pair(g)[::-1])
    body = functools.partial(
        _attention_body, pages_per_step=pps, n_items=n_items, blk=blk, lam_init=lam_init,
        sample_args=dict(n_steps=n_steps, n_groups=n_groups, past_len=n_pages * PAGE_SIZE,
                         dec_seq=dec_seq))
    grid_spec = pltpu.PrefetchScalarGridSpec(
        num_scalar_prefetch=1,
        grid=(n_groups + 1,),
        in_specs=[
            pl.BlockSpec((1, n_rows, V_DIM), score_row),
            pl.BlockSpec((1, n_rows, V_DIM), value_row),
            pl.BlockSpec(row_slope.shape, const2),
            *key_specs, *value_specs,
            pl.BlockSpec((1,) + knew.shape[1:], value_row),
            pl.BlockSpec((1,) + vnew.shape[1:], value_row),
            pl.BlockSpec(lq.shape, const2),
            pl.BlockSpec(gsub.shape, const2),
            head_rows, head_rows, head_cols,
            pl.BlockSpec(q_rows.shape, const2),
            pl.BlockSpec(k_rows.shape, const2),
        ],
        out_specs=[pl.BlockSpec((1, n_rows // 2, V_DIM), value_row), head_rows],
        scratch_shapes=[
            pltpu.VMEM((n_rows, PAGE_ROWS), F32),
            pltpu.VMEM((pps, n_rows, PAGE_ROWS), BF16),
            pltpu.VMEM((pps, n_rows, 1), F32),
            pltpu.VMEM((pps, n_rows, 1), F32),
            pltpu.VMEM((n_rows, 1), F32),
            pltpu.VMEM((n_rows, 1), F32),
            pltpu.VMEM((n_rows, V_DIM), F32),
            *_prompt_scratch(seq, blk),
        ],
    )
    o_s, o_p = pl.pallas_call(
        body,
        grid_spec=grid_spec,
        out_shape=[jax.ShapeDtypeStruct((dec_batch, n_rows // 2, V_DIM), F32),
                   jax.ShapeDtypeStruct(qb.shape, BF16)],
        compiler_params=pltpu.CompilerParams(
            dimension_semantics=("arbitrary",),
            vmem_limit_bytes=VMEM_LIMIT_BYTES),
        name="diff_attention",
    )(page_table.reshape(-1), qall, qall, row_slope, *([ck] * pps), *([cv] * pps),
      knew, vnew, lq, gsub, qb, kb, vtb, q_rows, k_rows)
    o_s = o_s.reshape(dec_batch, N_HEADS, dec_seq, V_DIM).transpose(0, 2, 1, 3)
    return o_p, o_s.reshape(dec_batch * dec_seq, N_HEADS * V_DIM)


def kernel(x_prompt, x_sample, state_pool, cache_k, cache_v, page_table, w_pool, pool_scale, g_kv,
           w_k, w_v, w_q, w_o, lambda_qk, g_sub, g_mix_pre, g_mix_post, g_ffn_pre, g_ffn_post,
           w_gate, w_up, w_down):
    batch, seq, d = x_prompt.shape
    dec_batch, dec_seq, _ = x_sample.shape
    past_len = page_table.shape[1] * PAGE_SIZE
    depth = g_mix_pre.shape[0]
    assert depth == 2 and w_pool.shape[0] == N_A_LAYERS and d == D_MODEL

    row = lambda a: a.reshape(1, -1)
    wpool = w_pool[0].astype(BF16)
    wg, wu, wd = (w.astype(BF16) for w in (w_gate, w_up, w_down))
    wk, wv, wq, wo = (w.astype(BF16) for w in (w_k, w_v, w_q[0], w_o[0]))
    slopes = jnp.exp2(-8.0 * (jnp.arange(N_HEADS, dtype=F32) + 1.0) / N_HEADS)
    lam_init = 0.8 - 0.6 * math.exp(-0.3 * 1)
    lq = lambda_qk[0].astype(F32)
    gsub = row(g_sub[0])

    def layer0(x, prev, pos0, tt, tm):
        b, t, _ = x.shape
        x1, state = _mixer(x, prev, wpool, row(pool_scale[0]), row(g_mix_pre[0]),
                           row(g_mix_post[0]), pos0=pos0, tt=tt)
        x2 = _ffn(x1.reshape(b * t, d), row(g_ffn_pre[0]), row(g_ffn_post[0]),
                  wg[0], wu[0], wd[0], tm=tm)
        return x2, state

    def layer1_tail(x2, o, tm):
        return _ffn(x2, row(g_ffn_pre[1]), row(g_ffn_post[1]), wg[1], wu[1], wd[1], tm=tm,
                    attn=(o, wo, row(g_mix_post[1])))

    zero_prev = jnp.zeros((batch, POOL_STATE_ROWS, d), F32)
    x2p, state_p = layer0(x_prompt, zero_prev, 0, TOKEN_TILE, TOKEN_TILE)
    kp, vp, _, kpb, vtpb, qpb = _proj(x2p, row(g_kv), row(g_mix_pre[1]), wk, wv, wq, tm=TOKEN_TILE)
    n_s = dec_batch * dec_seq
    x2s, state_s = layer0(x_sample, state_pool[0], past_len, dec_seq, n_s)
    ks, vs, qs, _, _, _ = _proj(x2s, row(g_kv), row(g_mix_pre[1]), wk, wv, wq, tm=n_s)

    op, os_ = _attention(qpb, kpb, vtpb, qs.reshape(dec_batch, dec_seq, -1),
                         ks.reshape(dec_batch, dec_seq, -1), vs.reshape(dec_batch, dec_seq, -1),
                         cache_k, cache_v, page_table, slopes, lq, gsub,
                         batch=batch, seq=seq, lam_init=lam_init)
    y_prompt = layer1_tail(x2p, op, TOKEN_TILE).reshape(batch, seq, d)
    y_sample = layer1_tail(x2s, os_, n_s).reshape(dec_batch, dec_seq, d)

    kv_shape_p = (batch, seq, N_HEADS, V_DIM)
    kv_shape_s = (dec_batch, dec_seq, N_HEADS, V_DIM)
    return (y_prompt, y_sample, state_p[None], state_s[None],
            kp.reshape(kv_shape_p), vp.reshape(kv_shape_p),
            ks.reshape(kv_shape_s), vs.reshape(kv_shape_s))
```

```python
import functools
import math

import jax
import jax.numpy as jnp
from jax import lax
from jax.experimental import pallas as pl
from jax.experimental.pallas import tpu as pltpu

F32 = jnp.float32
BF16 = jnp.bfloat16

D_MODEL = 1024
POOL_WINDOWS = (2, 4, 8, 16)
POOL_GROUP_DIM = D_MODEL // len(POOL_WINDOWS)
POOL_STATE_ROWS = max(POOL_WINDOWS) - 1
HALO_ROWS = POOL_STATE_ROWS + 1
HEAD_DIM = 64
V_DIM = 2 * HEAD_DIM
N_HEADS = D_MODEL // V_DIM
PAGE_SIZE = 128
PAGE_ROWS = PAGE_SIZE * N_HEADS
NORM_EPS = 1e-6
SUBLN_EPS = 1e-5
NEG_INF = -1e30
N_A_LAYERS = 1

VMEM_LIMIT_BYTES = 56 * 1024 * 1024
TOKEN_TILE = 512
ATTN_BLOCK = 256
PAGES_PER_STEP = 16
NEW_KEY_PAD = 16
LOG2E = math.log2(math.e)
POS_RADIX = 128
N_POS_FEATURES = 12


def _rms(xf, g, eps):
    return xf * lax.rsqrt(jnp.mean(xf * xf, axis=-1, keepdims=True) + eps) * g


def _const_spec(shape):
    zeros = (0,) * len(shape)
    return pl.BlockSpec(shape, lambda *_: zeros, pipeline_mode=pl.Buffered(1))


def _log2(n):
    assert n & (n - 1) == 0, n
    return n.bit_length() - 1


def _mixer_body(x_ref, prev_ref, wpool_ref, pscale_ref, gpre_ref, gpost_ref,
                out_ref, state_ref, ext_ref, y_ref, *, tt, n_tiles, pos0):
    j = pl.program_id(1)

    @pl.when(j == 0)
    def _():
        ext_ref[0:1, :] = jnp.zeros((1, D_MODEL), F32)
        ext_ref[1:HALO_ROWS, :] = prev_ref[0]

    x = x_ref[0]
    h = _rms(x, gpre_ref[...], NORM_EPS)
    ext_ref[HALO_ROWS:HALO_ROWS + tt, :] = h
    pos = pos0 + j * tt + lax.broadcasted_iota(jnp.int32, (tt, 1), 0)
    for g, w in enumerate(POOL_WINDOWS):
        cols = slice(g * POOL_GROUP_DIM, (g + 1) * POOL_GROUP_DIM)
        e = h[:, cols]
        s = e
        for k in range(1, w):
            s = s + ext_ref[HALO_ROWS - k:HALO_ROWS - k + tt, cols]
        inv_cnt = 1.0 / jnp.minimum(w, pos + 1).astype(F32)
        diff = (s * inv_cnt - e).astype(BF16)
        yg = jnp.dot(diff, wpool_ref[g], preferred_element_type=F32)
        y_ref[:, cols] = yg * pscale_ref[:, cols]
    out_ref[0] = x + _rms(y_ref[...], gpost_ref[...], NORM_EPS)

    @pl.when(j == n_tiles - 1)
    def _():
        state_ref[0] = ext_ref[tt + 1:tt + HALO_ROWS, :]

    if n_tiles > 1:
        ext_ref[0:HALO_ROWS, :] = ext_ref[tt:tt + HALO_ROWS, :]


def _mixer(x, prev, wpool, pscale, gpre, gpost, *, pos0, tt):
    b, t, d = x.shape
    n_tiles = t // tt
    body = functools.partial(_mixer_body, tt=tt, n_tiles=n_tiles, pos0=pos0)
    return pl.pallas_call(
        body,
        grid=(b, n_tiles),
        in_specs=[
            pl.BlockSpec((1, tt, d), lambda i, j: (i, j, 0)),
            pl.BlockSpec((1, POOL_STATE_ROWS, d), lambda i, j: (i, 0, 0)),
            _const_spec(wpool.shape),
            _const_spec((1, d)),
            _const_spec((1, d)),
            _const_spec((1, d)),
        ],
        out_specs=[
            pl.BlockSpec((1, tt, d), lambda i, j: (i, j, 0)),
            pl.BlockSpec((1, POOL_STATE_ROWS, d), lambda i, j: (i, 0, 0)),
        ],
        out_shape=[
            jax.ShapeDtypeStruct((b, t, d), F32),
            jax.ShapeDtypeStruct((b, POOL_STATE_ROWS, d), F32),
        ],
        scratch_shapes=[
            pltpu.VMEM((HALO_ROWS + tt, d), F32),
            pltpu.VMEM((tt, d), F32),
        ],
        compiler_params=pltpu.CompilerParams(
            dimension_semantics=("arbitrary", "arbitrary"),
            vmem_limit_bytes=VMEM_LIMIT_BYTES),
        name="pool_mixer",
    )(x, prev, wpool, pscale, gpre, gpost)


def _ffn_body(*refs, with_attn, ff_chunks):
    if with_attn:
        (x_ref, o_ref, wo_ref, gmix_ref, gpre_ref, gpost_ref, wg_ref, wu_ref, wd_ref,
         out_ref, act_ref) = refs
    else:
        x_ref, gpre_ref, gpost_ref, wg_ref, wu_ref, wd_ref, out_ref, act_ref = refs
    x = x_ref[...]
    if with_attn:
        y = jnp.dot(o_ref[...].astype(BF16), wo_ref[...], preferred_element_type=F32)
        x = x + _rms(y, gmix_ref[...], NORM_EPS)
    h = _rms(x, gpre_ref[...], NORM_EPS).astype(BF16)
    for c0, c1 in ff_chunks:
        gate = jnp.dot(h, wg_ref[:, c0:c1], preferred_element_type=F32)
        up = jnp.dot(h, wu_ref[:, c0:c1], preferred_element_type=F32)
        act_ref[:, c0:c1] = (gate * jax.nn.sigmoid(gate) * up).astype(BF16)
    f = jnp.dot(act_ref[...], wd_ref[...], preferred_element_type=F32)
    out_ref[...] = x + _rms(f, gpost_ref[...], NORM_EPS)


def _ffn(x, gpre, gpost, wg, wu, wd, *, tm, attn=None):
    n, d = x.shape
    d_ff = wg.shape[1]
    half = d_ff // 2
    ff_chunks = ((0, half), (half, d_ff)) if half % 128 == 0 else ((0, d_ff),)
    row_spec = pl.BlockSpec((tm, d), lambda i: (i, 0))
    args = [x]
    in_specs = [row_spec]
    if attn is not None:
        o, wo, gmix = attn
        args += [o, wo, gmix]
        in_specs += [pl.BlockSpec((tm, o.shape[1]), lambda i: (i, 0)),
                     _const_spec(wo.shape), _const_spec((1, d))]
    args += [gpre, gpost, wg, wu, wd]
    in_specs += [_const_spec((1, d)), _const_spec((1, d)),
                 _const_spec(wg.shape), _const_spec(wu.shape), _const_spec(wd.shape)]
    body = functools.partial(_ffn_body, with_attn=attn is not None, ff_chunks=ff_chunks)
    return pl.pallas_call(
        body,
        grid=(n // tm,),
        in_specs=in_specs,
        out_specs=row_spec,
        out_shape=jax.ShapeDtypeStruct((n, d), F32),
        scratch_shapes=[pltpu.VMEM((tm, d_ff), BF16)],
        compiler_params=pltpu.CompilerParams(
            dimension_semantics=("arbitrary",),
            vmem_limit_bytes=VMEM_LIMIT_BYTES),
        name="attn_out_ffn" if attn is not None else "ffn",
    )(*args)


def _proj_body(x_ref, gkv_ref, gq_ref, wk_ref, wv_ref, wq_ref,
               k_ref, v_ref, q_ref, kb_ref, vtb_ref, qb_ref):
    x = x_ref[...]
    src = _rms(x, gkv_ref[...], NORM_EPS).astype(BF16)
    k = jnp.dot(src, wk_ref[...], preferred_element_type=F32)
    k_ref[...] = k
    kb_ref[...] = k.astype(BF16)
    v = jnp.dot(src, wv_ref[...], preferred_element_type=F32)
    v_ref[...] = v
    vtb_ref[...] = v.T.astype(BF16)
    h = _rms(x, gq_ref[...], NORM_EPS).astype(BF16)
    q = jnp.dot(h, wq_ref[...], preferred_element_type=F32) * (HEAD_DIM ** -0.5)
    q_ref[...] = q
    qb_ref[...] = (q * LOG2E).astype(BF16)


def _proj(x, gkv, gq, wk, wv, wq, *, tm):
    n, d = x.shape
    w = wk.shape[1]
    row_in = pl.BlockSpec((tm, d), lambda i: (i, 0))
    row_out = pl.BlockSpec((tm, w), lambda i: (i, 0))
    col_out = pl.BlockSpec((w, tm), lambda i: (0, i))
    row_f32, row_bf16 = jax.ShapeDtypeStruct((n, w), F32), jax.ShapeDtypeStruct((n, w), BF16)
    return pl.pallas_call(
        _proj_body,
        grid=(n // tm,),
        in_specs=[row_in, _const_spec((1, d)), _const_spec((1, d)),
                  _const_spec(wk.shape), _const_spec(wv.shape), _const_spec(wq.shape)],
        out_specs=[row_out, row_out, row_out, row_out, col_out, row_out],
        out_shape=[row_f32, row_f32, row_f32, row_bf16,
                   jax.ShapeDtypeStruct((w, n), BF16), row_bf16],
        compiler_params=pltpu.CompilerParams(
            dimension_semantics=("arbitrary",),
            vmem_limit_bytes=VMEM_LIMIT_BYTES),
        name="kvq_proj",
    )(x, gkv, gq, wk, wv, wq)


def _lambda(lq_ref, lam_init):
    lq = lq_ref[...]
    a = jnp.sum(lq[0:1] * lq[1:2], axis=-1, keepdims=True)
    b = jnp.sum(lq[2:3] * lq[3:4], axis=-1, keepdims=True)
    return jnp.exp(a) - jnp.exp(b) + lam_init


def _diff_finalize(o1, o2, lq_ref, gsub_ref, lam_init):
    od = o1 - _lambda(lq_ref, lam_init) * o2
    return _rms(od, gsub_ref[...], SUBLN_EPS) * (1.0 - lam_init)


def _online_update(m_ref, l_ref, acc_ref, m_cur):
    m_prev = m_ref[...]
    m_new = jnp.maximum(m_prev, m_cur)
    alpha = jnp.exp(m_prev - m_new)
    m_ref[...] = m_new
    l_ref[...] = alpha * l_ref[...]
    acc_ref[...] = alpha * acc_ref[...]
    return m_new


def _alibi_feature_rows(slopes):
    cs = slopes * LOG2E
    c1 = cs.astype(BF16).astype(F32)
    c2 = (cs - c1).astype(BF16).astype(F32)
    c3 = (cs - c1 - c2).astype(BF16).astype(F32)
    c = jnp.stack([c1, c2, c3], axis=1)
    half = N_POS_FEATURES // 2
    pad = jnp.zeros((slopes.shape[0], V_DIM - N_POS_FEATURES), F32)
    zeros = jnp.zeros((slopes.shape[0], half), F32)
    q_rows = jnp.concatenate([POS_RADIX * c, c, zeros, pad], axis=1)
    k_rows = jnp.concatenate([zeros, -POS_RADIX * c, -c, pad], axis=1)
    return q_rows, k_rows


def _prompt_phases(item, q_ref, k_ref, vt_ref, qrow_ref, krow_ref, lq_ref, gsub_ref, o_ref,
                   pos_ref, mask_ref, kaug_ref, qfeat_ref, s_ref, p_ref, *, blk, lam_init):
    seq = q_ref.shape[0]
    nq = seq // blk
    n_slots = _prompt_slots(seq, blk)
    slot = item % n_slots
    head = (item // n_slots) % N_HEADS
    half = N_POS_FEATURES // 2
    contract_last = (((1,), (1,)), ((), ()))

    def init_tables():
        pos = lax.broadcasted_iota(jnp.int32, (seq, V_DIM), 0)
        lane = lax.broadcasted_iota(jnp.int32, (seq, V_DIM), 1)
        hi = (pos >> _log2(POS_RADIX)).astype(F32)
        lo = (pos & (POS_RADIX - 1)).astype(F32)
        third = half // 2
        is_hi = (lane < third) | ((lane >= half) & (lane < half + third))
        pos_ref[...] = jnp.where(lane >= N_POS_FEATURES, 0.0, jnp.where(is_hi, hi, lo))
        key = lax.broadcasted_iota(jnp.int32, (blk, 2 * blk), 0)
        qry = lax.broadcasted_iota(jnp.int32, (blk, 2 * blk), 1) & (blk - 1)
        mask_ref[...] = jnp.where(key <= qry, 0.0, NEG_INF)

    def prepare_head():
        lane = lax.broadcasted_iota(jnp.int32, (seq, V_DIM), 1)
        pos_feat = pos_ref[...]
        kaug_ref[:, 0:V_DIM] = k_ref[...]
        kaug_ref[:, V_DIM:] = jnp.where(lane < half, pos_feat,
                                        krow_ref[pl.ds(head, 1), :]).astype(BF16)
        qfeat_ref[...] = jnp.where(lane < half, qrow_ref[pl.ds(head, 1), :], pos_feat).astype(BF16)

    def query_block(i):
        lane = lax.broadcasted_iota(jnp.int32, (blk, V_DIM), 1)
        n_keys = (i + 1) * blk
        q = q_ref[i * blk:(i + 1) * blk, :]
        zero = jnp.zeros_like(q)
        qf = qfeat_ref[i * blk:(i + 1) * blk, :]
        q2 = jnp.concatenate(
            [jnp.concatenate([jnp.where(lane < HEAD_DIM, q, zero), qf], axis=1),
             jnp.concatenate([jnp.where(lane >= HEAD_DIM, q, zero), qf], axis=1)], axis=0)
        s_ref[0:n_keys, :] = lax.dot_general(kaug_ref[0:n_keys, :], q2, contract_last,
                                             preferred_element_type=F32)
        diag = slice(i * blk, (i + 1) * blk)
        s_ref[diag, :] = s_ref[diag, :] + mask_ref[...]
        m = None
        for j in range(i + 1):
            m_j = jnp.max(s_ref[j * blk:(j + 1) * blk, :], axis=0, keepdims=True)
            m = m_j if m is None else jnp.maximum(m, m_j)
        l = jnp.zeros_like(m)
        for j in range(i + 1):
            rows = slice(j * blk, (j + 1) * blk)
            p = jnp.exp2(s_ref[rows, :] - m)
            l = l + jnp.sum(p, axis=0, keepdims=True)
            p_ref[rows, :] = p.astype(BF16)
        ot = jnp.dot(vt_ref[:, 0:n_keys], p_ref[0:n_keys, :], preferred_element_type=F32) / l
        o = _diff_finalize(ot[:, :blk].T, ot[:, blk:].T, lq_ref, gsub_ref, lam_init)
        o_ref[i * blk:(i + 1) * blk, :] = o.astype(o_ref.dtype)

    def prepare():
        pl.when(item == 0)(init_tables)
        pl.when(slot == 0)(prepare_head)

    def query_blocks(j):
        query_block(j)
        query_block(nq - 1 - j)

    return prepare, slot, [functools.partial(query_blocks, j) for j in range(n_slots)]


def _prompt_slots(seq, blk):
    nq = seq // blk
    assert nq % 2 == 0, nq
    return nq // 2


def _prompt_scratch(seq, blk):
    return [
        pltpu.VMEM((seq, V_DIM), F32),
        pltpu.VMEM((blk, 2 * blk), F32),
        pltpu.VMEM((seq, 2 * V_DIM), BF16),
        pltpu.VMEM((seq, V_DIM), BF16),
        pltpu.VMEM((seq, 2 * blk), F32),
        pltpu.VMEM((seq, 2 * blk), BF16),
    ]


def _sample_phases(g, q_ref, qprev_ref, slope_ref, k_refs, v_refs, knew_ref, vnew_ref, lq_ref,
                   gsub_ref, o_ref, bias_ref, p_ref, mp_ref, lp_ref, m_ref, l_ref, acc_ref, *,
                   n_steps, n_groups, past_len, dec_seq, lam_init):
    pages_per_step = len(k_refs)
    prev_step = _value_group(g) % n_steps
    n_rows = q_ref.shape[1]
    slope = slope_ref[:, 0:1]
    contract_last = (((1,), (1,)), ((), ()))

    def row_col_ids(shape):
        r = lax.broadcasted_iota(jnp.int32, shape, 0)
        c = lax.broadcasted_iota(jnp.int32, shape, 1)
        head_q = (r >> _log2(dec_seq)) & (N_HEADS - 1)
        return head_q, r & (dec_seq - 1), c >> _log2(N_HEADS), c & (N_HEADS - 1)

    def init_tables():
        hq, tok, key, hk = row_col_ids((n_rows, PAGE_ROWS))
        bias_ref[...] = jnp.where(hq == hk, -slope * (tok - key).astype(F32), NEG_INF)
        p_ref[...] = jnp.zeros(p_ref.shape, p_ref.dtype)
        mp_ref[...] = jnp.full(mp_ref.shape, NEG_INF, F32)
        lp_ref[...] = jnp.zeros(lp_ref.shape, F32)

    def reset_row():
        m_ref[...] = jnp.full(m_ref.shape, NEG_INF, F32)
        l_ref[...] = jnp.zeros(l_ref.shape, F32)
        acc_ref[...] = jnp.zeros(acc_ref.shape, F32)

    def prepare():
        pl.when(g == 0)(init_tables)
        pl.when(prev_step == 0)(reset_row)

    def body():
        m_cur = mp_ref[0]
        for i in range(1, pages_per_step):
            m_cur = jnp.maximum(m_cur, mp_ref[i])
        m_new = _online_update(m_ref, l_ref, acc_ref, m_cur)
        l_add = jnp.zeros(l_ref.shape, F32)
        acc_add = jnp.zeros(acc_ref.shape, F32)
        for i in range(pages_per_step):
            w = jnp.exp(mp_ref[i] - m_new)
            l_add = l_add + w * lp_ref[i]
            acc_add = acc_add + w * jnp.dot(p_ref[i], v_refs[i][...], preferred_element_type=F32)
        l_ref[...] += l_add
        acc_ref[...] += acc_add

        q = q_ref[0]
        bias = bias_ref[...]
        step = _score_group(g, n_groups) % n_steps
        for i in range(pages_per_step):
            s = lax.dot_general(q, k_refs[i][...], contract_last, preferred_element_type=F32)
            t = s + bias
            m_page = jnp.max(t, axis=-1, keepdims=True)
            p = jnp.exp(t - m_page)
            p_ref[i] = p.astype(p_ref.dtype)
            lp_ref[i] = jnp.sum(p, axis=-1, keepdims=True)
            page_pos = (step * pages_per_step + i) * PAGE_SIZE
            mp_ref[i] = m_page - slope * (past_len - page_pos).astype(F32)

    def finish_row():
        qp = qprev_ref[0]
        hq, tok, key, hk = row_col_ids((n_rows, knew_ref.shape[1]))
        valid = (hq == hk) & (key <= tok)
        bias_new = jnp.where(valid, -slope * (tok - key).astype(F32), NEG_INF)
        s = lax.dot_general(qp, knew_ref[0], contract_last, preferred_element_type=F32)
        t = s + bias_new
        m_fin = _online_update(m_ref, l_ref, acc_ref, jnp.max(t, axis=-1, keepdims=True))
        p = jnp.exp(t - m_fin)
        l = l_ref[...] + jnp.sum(p, axis=-1, keepdims=True)
        acc = acc_ref[...] + jnp.dot(p, vnew_ref[0], preferred_element_type=F32)
        o = acc / l
        half = n_rows // 2
        o_ref[0] = _diff_finalize(o[:half], o[half:], lq_ref, gsub_ref, lam_init)

    def finish():
        pl.when((g > 0) & (prev_step == n_steps - 1))(finish_row)

    return prepare, body, finish


def _score_group(g, n_groups):
    return jnp.minimum(g, n_groups - 1)


def _value_group(g):
    return jnp.maximum(g - 1, 0)


def _key_page_index(g, pt, *, i, n_groups, pages_per_step):
    return (pt[_score_group(g, n_groups) * pages_per_step + i], 0)


def _value_page_index(g, pt, *, i, pages_per_step):
    return (pt[_value_group(g) * pages_per_step + i], 0)


def _attention_body(pt_ref, *refs, pages_per_step, n_items, blk, sample_args, lam_init):
    del pt_ref
    pps = pages_per_step
    q_ref, qprev_ref, slope_ref = refs[:3]
    k_refs = refs[3:3 + pps]
    v_refs = refs[3 + pps:3 + 2 * pps]
    (knew_ref, vnew_ref, lq_ref, gsub_ref, qb_ref, kb_ref, vtb_ref, qrow_ref, krow_ref,
     os_ref, op_ref) = refs[3 + 2 * pps:14 + 2 * pps]
    sample_scratch = refs[14 + 2 * pps:21 + 2 * pps]
    prompt_scratch = refs[21 + 2 * pps:]
    g = pl.program_id(0)
    sample_prepare, sample_body, sample_finish = _sample_phases(
        g, q_ref, qprev_ref, slope_ref, k_refs, v_refs, knew_ref, vnew_ref, lq_ref, gsub_ref,
        os_ref, *sample_scratch, lam_init=lam_init, **sample_args)
    has_item = g < n_items
    prompt_prepare, slot, slot_bodies = _prompt_phases(
        jnp.minimum(g, n_items - 1), qb_ref, kb_ref, vtb_ref, qrow_ref, krow_ref, lq_ref,
        gsub_ref, op_ref, *prompt_scratch, blk=blk, lam_init=lam_init)

    sample_prepare()
    pl.when(has_item)(prompt_prepare)

    def both(slot_body):
        sample_body()
        slot_body()

    for j, slot_body in enumerate(slot_bodies):
        pl.when(has_item & (slot == j))(functools.partial(both, slot_body))
    pl.when(jnp.logical_not(has_item))(sample_body)
    sample_finish()


def _attention(qb, kb, vtb, q_s, k_s, v_s, cache_k, cache_v, page_table, slopes, lq, gsub, *,
               batch, seq, lam_init):
    dec_batch, dec_seq, _ = q_s.shape
    n_pages = page_table.shape[1]
    pps = PAGES_PER_STEP
    n_steps = n_pages // pps
    n_groups = dec_batch * n_steps
    n_rows = 2 * N_HEADS * dec_seq
    blk = ATTN_BLOCK
    n_slots = _prompt_slots(seq, blk)
    n_items = batch * N_HEADS * n_slots
    assert n_items <= n_groups + 1, (n_items, n_groups)

    q5 = q_s.reshape(dec_batch, dec_seq, N_HEADS, 2, HEAD_DIM)
    zeros = jnp.zeros_like(q5[..., 0, :])
    qm = jnp.stack([jnp.concatenate([q5[..., 0, :], zeros], axis=-1),
                    jnp.concatenate([zeros, q5[..., 1, :]], axis=-1)], axis=1)
    qall = qm.transpose(0, 1, 3, 2, 4).reshape(dec_batch, n_rows, V_DIM)
    row_slope = jnp.broadcast_to(jnp.tile(jnp.repeat(slopes, dec_seq), 2)[:, None], (n_rows, 128))

    def pad_new(a):
        a = a.reshape(dec_batch, dec_seq, N_HEADS, V_DIM)
        a = jnp.pad(a, ((0, 0), (0, NEW_KEY_PAD - dec_seq), (0, 0), (0, 0)))
        return a.reshape(dec_batch, NEW_KEY_PAD * N_HEADS, V_DIM)

    knew, vnew = pad_new(k_s), pad_new(v_s)
    ck = cache_k.reshape(-1, V_DIM)
    cv = cache_v.reshape(-1, V_DIM)
    q_rows, k_rows = _alibi_feature_rows(slopes)

    key_specs = [
        pl.BlockSpec((PAGE_ROWS, V_DIM), functools.partial(
            _key_page_index, i=i, n_groups=n_groups, pages_per_step=pps))
        for i in range(pps)]
    value_specs = [
        pl.BlockSpec((PAGE_ROWS, V_DIM), functools.partial(
            _value_page_index, i=i, pages_per_step=pps))
        for i in range(pps)]
    score_row = lambda g, pt: (_score_group(g, n_groups) // n_steps, 0, 0)
    value_row = lambda g, pt: (_value_group(g) // n_steps, 0, 0)
    const2 = lambda g, pt: (0, 0)

    def pair(g):
        p = jnp.minimum(g, n_items - 1) // n_slots
        return p // N_HEADS, p % N_HEADS

    head_rows = pl.BlockSpec((seq, V_DIM), lambda g, pt: pair(g))
    head_cols = pl.BlockSpec((V_DIM, seq), lambda g, pt: pair(g)[::-1])
    body = functools.partial(
        _attention_body, pages_per_step=pps, n_items=n_items, blk=blk, lam_init=lam_init,
        sample_args=dict(n_steps=n_steps, n_groups=n_groups, past_len=n_pages * PAGE_SIZE,
                         dec_seq=dec_seq))
    grid_spec = pltpu.PrefetchScalarGridSpec(
        num_scalar_prefetch=1,
        grid=(n_groups + 1,),
        in_specs=[
            pl.BlockSpec((1, n_rows, V_DIM), score_row),
            pl.BlockSpec((1, n_rows, V_DIM), value_row),
            pl.BlockSpec(row_slope.shape, const2),
            *key_specs, *value_specs,
            pl.BlockSpec((1,) + knew.shape[1:], value_row),
            pl.BlockSpec((1,) + vnew.shape[1:], value_row),
            pl.BlockSpec(lq.shape, const2),
            pl.BlockSpec(gsub.shape, const2),
            head_rows, head_rows, head_cols,
            pl.BlockSpec(q_rows.shape, const2),
            pl.BlockSpec(k_rows.shape, const2),
        ],
        out_specs=[pl.BlockSpec((1, n_rows // 2, V_DIM), value_row), head_rows],
        scratch_shapes=[
            pltpu.VMEM((n_rows, PAGE_ROWS), F32),
            pltpu.VMEM((pps, n_rows, PAGE_ROWS), BF16),
            pltpu.VMEM((pps, n_rows, 1), F32),
            pltpu.VMEM((pps, n_rows, 1), F32),
            pltpu.VMEM((n_rows, 1), F32),
            pltpu.VMEM((n_rows, 1), F32),
            pltpu.VMEM((n_rows, V_DIM), F32),
            *_prompt_scratch(seq, blk),
        ],
    )
    o_s, o_p = pl.pallas_call(
        body,
        grid_spec=grid_spec,
        out_shape=[jax.ShapeDtypeStruct((dec_batch, n_rows // 2, V_DIM), F32),
                   jax.ShapeDtypeStruct(qb.shape, BF16)],
        compiler_params=pltpu.CompilerParams(
            dimension_semantics=("arbitrary",),
            vmem_limit_bytes=VMEM_LIMIT_BYTES),
        name="diff_attention",
    )(page_table.reshape(-1), qall, qall, row_slope, *([ck] * pps), *([cv] * pps),
      knew, vnew, lq, gsub, qb, kb, vtb, q_rows, k_rows)
    o_s = o_s.reshape(dec_batch, N_HEADS, dec_seq, V_DIM).transpose(0, 2, 1, 3)
    return o_p, o_s.reshape(dec_batch * dec_seq, N_HEADS * V_DIM)


def kernel(x_prompt, x_sample, state_pool, cache_k, cache_v, page_table, w_pool, pool_scale, g_kv,
           w_k, w_v, w_q, w_o, lambda_qk, g_sub, g_mix_pre, g_mix_post, g_ffn_pre, g_ffn_post,
           w_gate, w_up, w_down):
    batch, seq, d = x_prompt.shape
    dec_batch, dec_seq, _ = x_sample.shape
    past_len = page_table.shape[1] * PAGE_SIZE
    depth = g_mix_pre.shape[0]
    assert depth == 2 and w_pool.shape[0] == N_A_LAYERS and d == D_MODEL

    row = lambda a: a.reshape(1, -1)
    wpool = w_pool[0].astype(BF16)
    wg, wu, wd = (w.astype(BF16) for w in (w_gate, w_up, w_down))
    wk, wv, wq, wo = (w.astype(BF16) for w in (w_k, w_v, w_q[0], w_o[0]))
    slopes = jnp.exp2(-8.0 * (jnp.arange(N_HEADS, dtype=F32) + 1.0) / N_HEADS)
    lam_init = 0.8 - 0.6 * math.exp(-0.3 * 1)
    lq = lambda_qk[0].astype(F32)
    gsub = row(g_sub[0])

    def layer0(x, prev, pos0, tt, tm):
        b, t, _ = x.shape
        x1, state = _mixer(x, prev, wpool, row(pool_scale[0]), row(g_mix_pre[0]),
                           row(g_mix_post[0]), pos0=pos0, tt=tt)
        x2 = _ffn(x1.reshape(b * t, d), row(g_ffn_pre[0]), row(g_ffn_post[0]),
                  wg[0], wu[0], wd[0], tm=tm)
        return x2, state

    def layer1_tail(x2, o, tm):
        return _ffn(x2, row(g_ffn_pre[1]), row(g_ffn_post[1]), wg[1], wu[1], wd[1], tm=tm,
                    attn=(o, wo, row(g_mix_post[1])))

    zero_prev = jnp.zeros((batch, POOL_STATE_ROWS, d), F32)
    x2p, state_p = layer0(x_prompt, zero_prev, 0, TOKEN_TILE, TOKEN_TILE)
    kp, vp, _, kpb, vtpb, qpb = _proj(x2p, row(g_kv), row(g_mix_pre[1]), wk, wv, wq, tm=TOKEN_TILE)
    n_s = dec_batch * dec_seq
    x2s, state_s = layer0(x_sample, state_pool[0], past_len, dec_seq, n_s)
    ks, vs, qs, _, _, _ = _proj(x2s, row(g_kv), row(g_mix_pre[1]), wk, wv, wq, tm=n_s)

    op, os_ = _attention(qpb, kpb, vtpb, qs.reshape(dec_batch, dec_seq, -1),
                         ks.reshape(dec_batch, dec_seq, -1), vs.reshape(dec_batch, dec_seq, -1),
                         cache_k, cache_v, page_table, slopes, lq, gsub,
                         batch=batch, seq=seq, lam_init=lam_init)
    y_prompt = layer1_tail(x2p, op, TOKEN_TILE).reshape(batch, seq, d)
    y_sample = layer1_tail(x2s, os_, n_s).reshape(dec_batch, dec_seq, d)

    kv_shape_p = (batch, seq, N_HEADS, V_DIM)
    kv_shape_s = (dec_batch, dec_seq, N_HEADS, V_DIM)
    return (y_prompt, y_sample, state_p[None], state_s[None],
            kp.reshape(kv_shape_p), vp.reshape(kv_shape_p),
            ks.reshape(kv_shape_s), vs.reshape(kv_shape_s))
```

```python
import functools
import math

import jax
import jax.numpy as jnp
from jax import lax
from jax.experimental import pallas as pl
from jax.experimental.pallas import tpu as pltpu

F32 = jnp.float32
BF16 = jnp.bfloat16

D_MODEL = 1024
POOL_WINDOWS = (2, 4, 8, 16)
POOL_GROUP_DIM = D_MODEL // len(POOL_WINDOWS)
POOL_STATE_ROWS = max(POOL_WINDOWS) - 1
HALO_ROWS = POOL_STATE_ROWS + 1
HEAD_DIM = 64
V_DIM = 2 * HEAD_DIM
N_HEADS = D_MODEL // V_DIM
PAGE_SIZE = 128
PAGE_ROWS = PAGE_SIZE * N_HEADS
NORM_EPS = 1e-6
SUBLN_EPS = 1e-5
NEG_INF = -1e30
N_A_LAYERS = 1

VMEM_LIMIT_BYTES = 56 * 1024 * 1024
TOKEN_TILE = 512
ATTN_BLOCK = 256
PAGES_PER_STEP = 8
PAGE_RING = 3
N_SAMPLE_SCRATCH = 7
NEW_KEY_PAD = 16
LOG2E = math.log2(math.e)
POS_RADIX = 128
N_POS_FEATURES = 12


def _rms(xf, g, eps):
    return xf * lax.rsqrt(jnp.mean(xf * xf, axis=-1, keepdims=True) + eps) * g


def _const_spec(shape):
    zeros = (0,) * len(shape)
    return pl.BlockSpec(shape, lambda *_: zeros, pipeline_mode=pl.Buffered(1))


def _log2(n):
    assert n & (n - 1) == 0, n
    return n.bit_length() - 1


def _mixer_body(x_ref, prev_ref, wpool_ref, pscale_ref, gpre_ref, gpost_ref,
                out_ref, state_ref, ext_ref, y_ref, *, tt, n_tiles, pos0):
    j = pl.program_id(1)

    @pl.when(j == 0)
    def _():
        ext_ref[0:1, :] = jnp.zeros((1, D_MODEL), F32)
        ext_ref[1:HALO_ROWS, :] = prev_ref[0]

    x = x_ref[0]
    h = _rms(x, gpre_ref[...], NORM_EPS)
    ext_ref[HALO_ROWS:HALO_ROWS + tt, :] = h
    pos = pos0 + j * tt + lax.broadcasted_iota(jnp.int32, (tt, 1), 0)
    for g, w in enumerate(POOL_WINDOWS):
        cols = slice(g * POOL_GROUP_DIM, (g + 1) * POOL_GROUP_DIM)
        e = h[:, cols]
        s = e
        for k in range(1, w):
            s = s + ext_ref[HALO_ROWS - k:HALO_ROWS - k + tt, cols]
        inv_cnt = 1.0 / jnp.minimum(w, pos + 1).astype(F32)
        diff = (s * inv_cnt - e).astype(BF16)
        yg = jnp.dot(diff, wpool_ref[g], preferred_element_type=F32)
        y_ref[:, cols] = yg * pscale_ref[:, cols]
    out_ref[0] = x + _rms(y_ref[...], gpost_ref[...], NORM_EPS)

    @pl.when(j == n_tiles - 1)
    def _():
        state_ref[0] = ext_ref[tt + 1:tt + HALO_ROWS, :]

    if n_tiles > 1:
        ext_ref[0:HALO_ROWS, :] = ext_ref[tt:tt + HALO_ROWS, :]


def _mixer(x, prev, wpool, pscale, gpre, gpost, *, pos0, tt):
    b, t, d = x.shape
    n_tiles = t // tt
    body = functools.partial(_mixer_body, tt=tt, n_tiles=n_tiles, pos0=pos0)
    return pl.pallas_call(
        body,
        grid=(b, n_tiles),
        in_specs=[
            pl.BlockSpec((1, tt, d), lambda i, j: (i, j, 0)),
            pl.BlockSpec((1, POOL_STATE_ROWS, d), lambda i, j: (i, 0, 0)),
            _const_spec(wpool.shape),
            _const_spec((1, d)),
            _const_spec((1, d)),
            _const_spec((1, d)),
        ],
        out_specs=[
            pl.BlockSpec((1, tt, d), lambda i, j: (i, j, 0)),
            pl.BlockSpec((1, POOL_STATE_ROWS, d), lambda i, j: (i, 0, 0)),
        ],
        out_shape=[
            jax.ShapeDtypeStruct((b, t, d), F32),
            jax.ShapeDtypeStruct((b, POOL_STATE_ROWS, d), F32),
        ],
        scratch_shapes=[
            pltpu.VMEM((HALO_ROWS + tt, d), F32),
            pltpu.VMEM((tt, d), F32),
        ],
        compiler_params=pltpu.CompilerParams(
            dimension_semantics=("arbitrary", "arbitrary"),
            vmem_limit_bytes=VMEM_LIMIT_BYTES),
        name="pool_mixer",
    )(x, prev, wpool, pscale, gpre, gpost)


def _ffn_body(*refs, with_attn, ff_chunks):
    if with_attn:
        (x_ref, o_ref, wo_ref, gmix_ref, gpre_ref, gpost_ref, wg_ref, wu_ref, wd_ref,
         out_ref, act_ref) = refs
    else:
        x_ref, gpre_ref, gpost_ref, wg_ref, wu_ref, wd_ref, out_ref, act_ref = refs
    x = x_ref[...]
    if with_attn:
        y = jnp.dot(o_ref[...].astype(BF16), wo_ref[...], preferred_element_type=F32)
        x = x + _rms(y, gmix_ref[...], NORM_EPS)
    h = _rms(x, gpre_ref[...], NORM_EPS).astype(BF16)
    for c0, c1 in ff_chunks:
        gate = jnp.dot(h, wg_ref[:, c0:c1], preferred_element_type=F32)
        up = jnp.dot(h, wu_ref[:, c0:c1], preferred_element_type=F32)
        act_ref[:, c0:c1] = (gate * jax.nn.sigmoid(gate) * up).astype(BF16)
    f = jnp.dot(act_ref[...], wd_ref[...], preferred_element_type=F32)
    out_ref[...] = x + _rms(f, gpost_ref[...], NORM_EPS)


def _layer_spec(stack, layer):
    return pl.BlockSpec((None,) + stack.shape[1:], lambda *_: (layer, 0, 0),
                        pipeline_mode=pl.Buffered(1))


def _ffn(x, gpre, gpost, wg, wu, wd, *, layer, tm, attn=None):
    n, d = x.shape
    d_ff = wg.shape[2]
    half = d_ff // 2
    ff_chunks = ((0, half), (half, d_ff)) if half % 128 == 0 else ((0, d_ff),)
    row_spec = pl.BlockSpec((tm, d), lambda i: (i, 0))
    args = [x]
    in_specs = [row_spec]
    if attn is not None:
        o, wo, gmix = attn
        args += [o, wo, gmix]
        in_specs += [pl.BlockSpec((tm, o.shape[1]), lambda i: (i, 0)),
                     _const_spec(wo.shape), _const_spec((1, d))]
    args += [gpre, gpost, wg, wu, wd]
    in_specs += [_const_spec((1, d)), _const_spec((1, d)),
                 _layer_spec(wg, layer), _layer_spec(wu, layer), _layer_spec(wd, layer)]
    body = functools.partial(_ffn_body, with_attn=attn is not None, ff_chunks=ff_chunks)
    return pl.pallas_call(
        body,
        grid=(n // tm,),
        in_specs=in_specs,
        out_specs=row_spec,
        out_shape=jax.ShapeDtypeStruct((n, d), F32),
        scratch_shapes=[pltpu.VMEM((tm, d_ff), BF16)],
        compiler_params=pltpu.CompilerParams(
            dimension_semantics=("arbitrary",),
            vmem_limit_bytes=VMEM_LIMIT_BYTES),
        name="attn_out_ffn" if attn is not None else "ffn",
    )(*args)


def _proj_body(x_ref, gkv_ref, gq_ref, wk_ref, wv_ref, wq_ref,
               k_ref, v_ref, q_ref, kb_ref, vtb_ref, qb_ref):
    x = x_ref[...]
    src = _rms(x, gkv_ref[...], NORM_EPS).astype(BF16)
    k = jnp.dot(src, wk_ref[...], preferred_element_type=F32)
    k_ref[...] = k
    kb_ref[...] = k.astype(BF16)
    v = jnp.dot(src, wv_ref[...], preferred_element_type=F32)
    v_ref[...] = v
    vtb_ref[...] = v.T.astype(BF16)
    h = _rms(x, gq_ref[...], NORM_EPS).astype(BF16)
    q = jnp.dot(h, wq_ref[...], preferred_element_type=F32) * (HEAD_DIM ** -0.5)
    q_ref[...] = q
    qb_ref[...] = (q * LOG2E).astype(BF16)


def _proj(x, gkv, gq, wk, wv, wq, *, tm):
    n, d = x.shape
    w = wk.shape[1]
    row_in = pl.BlockSpec((tm, d), lambda i: (i, 0))
    row_out = pl.BlockSpec((tm, w), lambda i: (i, 0))
    col_out = pl.BlockSpec((w, tm), lambda i: (0, i))
    row_f32, row_bf16 = jax.ShapeDtypeStruct((n, w), F32), jax.ShapeDtypeStruct((n, w), BF16)
    return pl.pallas_call(
        _proj_body,
        grid=(n // tm,),
        in_specs=[row_in, _const_spec((1, d)), _const_spec((1, d)),
                  _const_spec(wk.shape), _const_spec(wv.shape), _const_spec(wq.shape)],
        out_specs=[row_out, row_out, row_out, row_out, col_out, row_out],
        out_shape=[row_f32, row_f32, row_f32, row_bf16,
                   jax.ShapeDtypeStruct((w, n), BF16), row_bf16],
        compiler_params=pltpu.CompilerParams(
            dimension_semantics=("arbitrary",),
            vmem_limit_bytes=VMEM_LIMIT_BYTES),
        name="kvq_proj",
    )(x, gkv, gq, wk, wv, wq)


def _lambda(lq_ref, lam_init):
    lq = lq_ref[...]
    a = jnp.sum(lq[0:1] * lq[1:2], axis=-1, keepdims=True)
    b = jnp.sum(lq[2:3] * lq[3:4], axis=-1, keepdims=True)
    return jnp.exp(a) - jnp.exp(b) + lam_init


def _diff_finalize(o1, o2, lq_ref, gsub_ref, lam_init):
    od = o1 - _lambda(lq_ref, lam_init) * o2
    return _rms(od, gsub_ref[...], SUBLN_EPS) * (1.0 - lam_init)


def _online_update(m_ref, l_ref, acc_ref, m_cur):
    m_prev = m_ref[...]
    m_new = jnp.maximum(m_prev, m_cur)
    alpha = jnp.exp(m_prev - m_new)
    m_ref[...] = m_new
    l_ref[...] = alpha * l_ref[...]
    acc_ref[...] = alpha * acc_ref[...]
    return m_new


def _alibi_feature_rows(slopes):
    cs = slopes * LOG2E
    c1 = cs.astype(BF16).astype(F32)
    c2 = (cs - c1).astype(BF16).astype(F32)
    c3 = (cs - c1 - c2).astype(BF16).astype(F32)
    c = jnp.stack([c1, c2, c3], axis=1)
    half = N_POS_FEATURES // 2
    pad = jnp.zeros((slopes.shape[0], V_DIM - N_POS_FEATURES), F32)
    zeros = jnp.zeros((slopes.shape[0], half), F32)
    q_rows = jnp.concatenate([POS_RADIX * c, c, zeros, pad], axis=1)
    k_rows = jnp.concatenate([zeros, -POS_RADIX * c, -c, pad], axis=1)
    return q_rows, k_rows


def _prompt_phases(item, q_ref, k_ref, vt_ref, qrow_ref, krow_ref, lq_ref, gsub_ref, o_ref,
                   pos_ref, mask_ref, kaug_ref, qfeat_ref, s_ref, p_ref, *, blk, lam_init):
    seq = q_ref.shape[0]
    nq = seq // blk
    slot = item % nq
    head = (item // nq) % N_HEADS
    half = N_POS_FEATURES // 2
    contract_last = (((1,), (1,)), ((), ()))

    def init_tables():
        pos = lax.broadcasted_iota(jnp.int32, (seq, V_DIM), 0)
        lane = lax.broadcasted_iota(jnp.int32, (seq, V_DIM), 1)
        hi = (pos >> _log2(POS_RADIX)).astype(F32)
        lo = (pos & (POS_RADIX - 1)).astype(F32)
        third = half // 2
        is_hi = (lane < third) | ((lane >= half) & (lane < half + third))
        pos_ref[...] = jnp.where(lane >= N_POS_FEATURES, 0.0, jnp.where(is_hi, hi, lo))
        key = lax.broadcasted_iota(jnp.int32, (blk, 2 * blk), 0)
        qry = lax.broadcasted_iota(jnp.int32, (blk, 2 * blk), 1) & (blk - 1)
        mask_ref[...] = jnp.where(key <= qry, 0.0, NEG_INF)

    def prepare_head():
        lane = lax.broadcasted_iota(jnp.int32, (seq, V_DIM), 1)
        pos_feat = pos_ref[...]
        kaug_ref[:, 0:V_DIM] = k_ref[...]
        kaug_ref[:, V_DIM:] = jnp.where(lane < half, pos_feat,
                                        krow_ref[pl.ds(head, 1), :]).astype(BF16)
        qfeat_ref[...] = jnp.where(lane < half, qrow_ref[pl.ds(head, 1), :], pos_feat).astype(BF16)

    def query_block(i):
        lane = lax.broadcasted_iota(jnp.int32, (blk, V_DIM), 1)
        n_keys = (i + 1) * blk
        q = q_ref[i * blk:(i + 1) * blk, :]
        zero = jnp.zeros_like(q)
        qf = qfeat_ref[i * blk:(i + 1) * blk, :]
        q2 = jnp.concatenate(
            [jnp.concatenate([jnp.where(lane < HEAD_DIM, q, zero), qf], axis=1),
             jnp.concatenate([jnp.where(lane >= HEAD_DIM, q, zero), qf], axis=1)], axis=0)
        s_ref[0:n_keys, :] = lax.dot_general(kaug_ref[0:n_keys, :], q2, contract_last,
                                             preferred_element_type=F32)
        diag = slice(i * blk, (i + 1) * blk)
        s_ref[diag, :] = s_ref[diag, :] + mask_ref[...]
        m = None
        for j in range(i + 1):
            m_j = jnp.max(s_ref[j * blk:(j + 1) * blk, :], axis=0, keepdims=True)
            m = m_j if m is None else jnp.maximum(m, m_j)
        l = jnp.zeros_like(m)
        for j in range(i + 1):
            rows = slice(j * blk, (j + 1) * blk)
            p = jnp.exp2(s_ref[rows, :] - m)
            l = l + jnp.sum(p, axis=0, keepdims=True)
            p_ref[rows, :] = p.astype(BF16)
        ot = jnp.dot(vt_ref[:, 0:n_keys], p_ref[0:n_keys, :], preferred_element_type=F32) / l
        o = _diff_finalize(ot[:, :blk].T, ot[:, blk:].T, lq_ref, gsub_ref, lam_init)
        o_ref[i * blk:(i + 1) * blk, :] = o.astype(o_ref.dtype)

    def prepare():
        pl.when(item == 0)(init_tables)
        pl.when(slot == 0)(prepare_head)

    return prepare, slot, [functools.partial(query_block, i) for i in range(nq)]


def _prompt_scratch(seq, blk):
    return [
        pltpu.VMEM((seq, V_DIM), F32),
        pltpu.VMEM((blk, 2 * blk), F32),
        pltpu.VMEM((seq, 2 * V_DIM), BF16),
        pltpu.VMEM((seq, V_DIM), BF16),
        pltpu.VMEM((seq, 2 * blk), F32),
        pltpu.VMEM((seq, 2 * blk), BF16),
    ]


def _sample_phases(g, q_ref, qprev_ref, slope_ref, k_refs, v_refs, knew_ref, vnew_ref, lq_ref,
                   gsub_ref, o_ref, bias_ref, p_ref, mp_ref, lp_ref, m_ref, l_ref, acc_ref, *,
                   n_steps, n_groups, past_len, dec_seq, lam_init):
    pages_per_step = len(k_refs)
    prev_step = _value_group(g) % n_steps
    n_rows = q_ref.shape[1]
    slope = slope_ref[:, 0:1]
    contract_last = (((1,), (1,)), ((), ()))

    def row_col_ids(shape):
        r = lax.broadcasted_iota(jnp.int32, shape, 0)
        c = lax.broadcasted_iota(jnp.int32, shape, 1)
        head_q = (r >> _log2(dec_seq)) & (N_HEADS - 1)
        return head_q, r & (dec_seq - 1), c >> _log2(N_HEADS), c & (N_HEADS - 1)

    def init_tables():
        hq, tok, key, hk = row_col_ids((n_rows, PAGE_ROWS))
        bias_ref[...] = jnp.where(hq == hk, -slope * (tok - key).astype(F32), NEG_INF)
        p_ref[...] = jnp.zeros(p_ref.shape, p_ref.dtype)
        mp_ref[...] = jnp.full(mp_ref.shape, NEG_INF, F32)
        lp_ref[...] = jnp.zeros(lp_ref.shape, F32)

    def reset_row():
        m_ref[...] = jnp.full(m_ref.shape, NEG_INF, F32)
        l_ref[...] = jnp.zeros(l_ref.shape, F32)
        acc_ref[...] = jnp.zeros(acc_ref.shape, F32)

    def prepare():
        pl.when(g == 0)(init_tables)
        pl.when(prev_step == 0)(reset_row)

    def body():
        m_cur = mp_ref[0]
        for i in range(1, pages_per_step):
            m_cur = jnp.maximum(m_cur, mp_ref[i])
        m_new = _online_update(m_ref, l_ref, acc_ref, m_cur)
        l_add = jnp.zeros(l_ref.shape, F32)
        acc_add = jnp.zeros(acc_ref.shape, F32)
        for i in range(pages_per_step):
            w = jnp.exp(mp_ref[i] - m_new)
            l_add = l_add + w * lp_ref[i]
            acc_add = acc_add + w * jnp.dot(p_ref[i], v_refs[i][...], preferred_element_type=F32)
        l_ref[...] += l_add
        acc_ref[...] += acc_add

        q = q_ref[0]
        bias = bias_ref[...]
        step = _score_group(g, n_groups) % n_steps
        for i in range(pages_per_step):
            s = lax.dot_general(q, k_refs[i][...], contract_last, preferred_element_type=F32)
            t = s + bias
            m_page = jnp.max(t, axis=-1, keepdims=True)
            p = jnp.exp(t - m_page)
            p_ref[i] = p.astype(p_ref.dtype)
            lp_ref[i] = jnp.sum(p, axis=-1, keepdims=True)
            page_pos = (step * pages_per_step + i) * PAGE_SIZE
            mp_ref[i] = m_page - slope * (past_len - page_pos).astype(F32)

    def finish_row():
        qp = qprev_ref[0]
        hq, tok, key, hk = row_col_ids((n_rows, knew_ref.shape[1]))
        valid = (hq == hk) & (key <= tok)
        bias_new = jnp.where(valid, -slope * (tok - key).astype(F32), NEG_INF)
        s = lax.dot_general(qp, knew_ref[0], contract_last, preferred_element_type=F32)
        t = s + bias_new
        m_fin = _online_update(m_ref, l_ref, acc_ref, jnp.max(t, axis=-1, keepdims=True))
        p = jnp.exp(t - m_fin)
        l = l_ref[...] + jnp.sum(p, axis=-1, keepdims=True)
        acc = acc_ref[...] + jnp.dot(p, vnew_ref[0], preferred_element_type=F32)
        o = acc / l
        half = n_rows // 2
        o_ref[0] = _diff_finalize(o[:half], o[half:], lq_ref, gsub_ref, lam_init)

    def finish():
        pl.when((g > 0) & (prev_step == n_steps - 1))(finish_row)

    return prepare, body, finish


def _score_group(g, n_groups):
    return jnp.minimum(g, n_groups - 1)


def _value_group(g):
    return jnp.maximum(g - 1, 0)


def _attention_body(pt_ref, q_ref, qprev_ref, slope_ref, ck_ref, cv_ref, knew_ref, vnew_ref, lq_ref,
                    gsub_ref, qb_ref, kb_ref, vtb_ref, qrow_ref, krow_ref, os_ref, op_ref,
                    kbuf_ref, vbuf_ref, ksem, vsem, *scratch, pages_per_step, n_groups, n_items,
                    blk, sample_args, lam_init):
    sample_scratch = scratch[:N_SAMPLE_SCRATCH]
    prompt_scratch = scratch[N_SAMPLE_SCRATCH:]
    g = pl.program_id(0)

    def page_copies(cache_ref, buf_ref, sem, group):
        slot = group % PAGE_RING
        copies = []
        for i in range(pages_per_step):
            row0 = pl.multiple_of(pt_ref[group * pages_per_step + i] * PAGE_ROWS, PAGE_ROWS)
            copies.append(pltpu.make_async_copy(
                cache_ref.at[pl.ds(row0, PAGE_ROWS), :], buf_ref.at[slot, i], sem.at[slot]))
        return copies

    def start_keys(group):
        for copy in page_copies(ck_ref, kbuf_ref, ksem, group):
            copy.start()

    def start_values(group):
        for copy in page_copies(cv_ref, vbuf_ref, vsem, group):
            copy.start()

    @pl.when(g == 0)
    def _():
        start_keys(0)
        start_values(0)
        if n_groups > 1:
            start_keys(1)
        vbuf_ref[PAGE_RING - 1] = jnp.zeros(vbuf_ref.shape[1:], vbuf_ref.dtype)

    pl.when(g + 2 < n_groups)(lambda: start_keys(g + 2))
    pl.when(g + 1 < n_groups)(lambda: start_values(g + 1))

    @pl.when(g < n_groups)
    def _():
        for copy in page_copies(ck_ref, kbuf_ref, ksem, g):
            copy.wait()

    @pl.when(g > 0)
    def _():
        for copy in page_copies(cv_ref, vbuf_ref, vsem, g - 1):
            copy.wait()

    key_slot = g % PAGE_RING
    value_slot = (g + PAGE_RING - 1) % PAGE_RING
    k_refs = [kbuf_ref.at[key_slot, i] for i in range(pages_per_step)]
    v_refs = [vbuf_ref.at[value_slot, i] for i in range(pages_per_step)]
    sample_prepare, sample_body, sample_finish = _sample_phases(
        g, q_ref, qprev_ref, slope_ref, k_refs, v_refs, knew_ref, vnew_ref, lq_ref, gsub_ref,
        os_ref, *sample_scratch, lam_init=lam_init, **sample_args)
    has_item = g < n_items
    prompt_prepare, slot, slot_bodies = _prompt_phases(
        jnp.minimum(g, n_items - 1), qb_ref, kb_ref, vtb_ref, qrow_ref, krow_ref, lq_ref,
        gsub_ref, op_ref, *prompt_scratch, blk=blk, lam_init=lam_init)

    sample_prepare()
    pl.when(has_item)(prompt_prepare)

    def both(slot_body):
        sample_body()
        slot_body()

    for j, slot_body in enumerate(slot_bodies):
        pl.when(has_item & (slot == j))(functools.partial(both, slot_body))
    pl.when(jnp.logical_not(has_item))(sample_body)
    sample_finish()


def _attention(qb, kb, vtb, q_s, k_s, v_s, cache_k, cache_v, page_table, slopes, lq, gsub, *,
               batch, seq, lam_init):
    dec_batch, dec_seq, _ = q_s.shape
    n_pages = page_table.shape[1]
    pps = PAGES_PER_STEP
    n_steps = n_pages // pps
    n_groups = dec_batch * n_steps
    n_rows = 2 * N_HEADS * dec_seq
    blk = ATTN_BLOCK
    n_slots = seq // blk
    n_items = batch * N_HEADS * n_slots
    assert n_items <= n_groups + 1, (n_items, n_groups)

    q5 = q_s.reshape(dec_batch, dec_seq, N_HEADS, 2, HEAD_DIM)
    zeros = jnp.zeros_like(q5[..., 0, :])
    qm = jnp.stack([jnp.concatenate([q5[..., 0, :], zeros], axis=-1),
                    jnp.concatenate([zeros, q5[..., 1, :]], axis=-1)], axis=1)
    qall = qm.transpose(0, 1, 3, 2, 4).reshape(dec_batch, n_rows, V_DIM)
    row_slope = jnp.broadcast_to(jnp.tile(jnp.repeat(slopes, dec_seq), 2)[:, None], (n_rows, 128))

    def pad_new(a):
        a = a.reshape(dec_batch, dec_seq, N_HEADS, V_DIM)
        a = jnp.pad(a, ((0, 0), (0, NEW_KEY_PAD - dec_seq), (0, 0), (0, 0)))
        return a.reshape(dec_batch, NEW_KEY_PAD * N_HEADS, V_DIM)

    knew, vnew = pad_new(k_s), pad_new(v_s)
    ck = cache_k.reshape(-1, V_DIM)
    cv = cache_v.reshape(-1, V_DIM)
    q_rows, k_rows = _alibi_feature_rows(slopes)

    score_row = lambda g, pt: (_score_group(g, n_groups) // n_steps, 0, 0)
    value_row = lambda g, pt: (_value_group(g) // n_steps, 0, 0)
    const2 = lambda g, pt: (0, 0)

    def pair(g):
        p = jnp.minimum(g, n_items - 1) // n_slots
        return p // N_HEADS, p % N_HEADS

    head_rows = pl.BlockSpec((seq, V_DIM), lambda g, pt: pair(g))
    head_cols = pl.BlockSpec((V_DIM, seq), lambda g, pt: pair(g)[::-1])
    cache_spec = pl.BlockSpec(memory_space=pl.ANY)
    page_ring = pltpu.VMEM((PAGE_RING, pps, PAGE_ROWS, V_DIM), cache_k.dtype)
    sample_scratch = [
        pltpu.VMEM((n_rows, PAGE_ROWS), F32),
        pltpu.VMEM((pps, n_rows, PAGE_ROWS), BF16),
        pltpu.VMEM((pps, n_rows, 1), F32),
        pltpu.VMEM((pps, n_rows, 1), F32),
        pltpu.VMEM((n_rows, 1), F32),
        pltpu.VMEM((n_rows, 1), F32),
        pltpu.VMEM((n_rows, V_DIM), F32),
    ]
    assert len(sample_scratch) == N_SAMPLE_SCRATCH
    body = functools.partial(
        _attention_body, pages_per_step=pps, n_groups=n_groups, n_items=n_items, blk=blk,
        lam_init=lam_init,
        sample_args=dict(n_steps=n_steps, n_groups=n_groups, past_len=n_pages * PAGE_SIZE,
                         dec_seq=dec_seq))
    grid_spec = pltpu.PrefetchScalarGridSpec(
        num_scalar_prefetch=1,
        grid=(n_groups + 1,),
        in_specs=[
            pl.BlockSpec((1, n_rows, V_DIM), score_row),
            pl.BlockSpec((1, n_rows, V_DIM), value_row),
            pl.BlockSpec(row_slope.shape, const2),
            cache_spec, cache_spec,
            pl.BlockSpec((1,) + knew.shape[1:], value_row),
            pl.BlockSpec((1,) + vnew.shape[1:], value_row),
            pl.BlockSpec(lq.shape, const2),
            pl.BlockSpec(gsub.shape, const2),
            head_rows, head_rows, head_cols,
            pl.BlockSpec(q_rows.shape, const2),
            pl.BlockSpec(k_rows.shape, const2),
        ],
        out_specs=[pl.BlockSpec((1, n_rows // 2, V_DIM), value_row), head_rows],
        scratch_shapes=[
            page_ring, page_ring,
            pltpu.SemaphoreType.DMA((PAGE_RING,)), pltpu.SemaphoreType.DMA((PAGE_RING,)),
            *sample_scratch, *_prompt_scratch(seq, blk),
        ],
    )
    o_s, o_p = pl.pallas_call(
        body,
        grid_spec=grid_spec,
        out_shape=[jax.ShapeDtypeStruct((dec_batch, n_rows // 2, V_DIM), F32),
                   jax.ShapeDtypeStruct(qb.shape, BF16)],
        compiler_params=pltpu.CompilerParams(
            dimension_semantics=("arbitrary",),
            vmem_limit_bytes=VMEM_LIMIT_BYTES),
        name="diff_attention",
    )(page_table.reshape(-1), qall, qall, row_slope, ck, cv,
      knew, vnew, lq, gsub, qb, kb, vtb, q_rows, k_rows)
    o_s = o_s.reshape(dec_batch, N_HEADS, dec_seq, V_DIM).transpose(0, 2, 1, 3)
    return o_p, o_s.reshape(dec_batch * dec_seq, N_HEADS * V_DIM)


def kernel(x_prompt, x_sample, state_pool, cache_k, cache_v, page_table, w_pool, pool_scale, g_kv,
           w_k, w_v, w_q, w_o, lambda_qk, g_sub, g_mix_pre, g_mix_post, g_ffn_pre, g_ffn_post,
           w_gate, w_up, w_down):
    batch, seq, d = x_prompt.shape
    dec_batch, dec_seq, _ = x_sample.shape
    past_len = page_table.shape[1] * PAGE_SIZE
    depth = g_mix_pre.shape[0]
    assert depth == 2 and w_pool.shape[0] == N_A_LAYERS and d == D_MODEL

    row = lambda a: a.reshape(1, -1)
    wpool = w_pool[0].astype(BF16)
    wg, wu, wd = (w.astype(BF16) for w in (w_gate, w_up, w_down))
    wk, wv, wq, wo = (w.astype(BF16) for w in (w_k, w_v, w_q[0], w_o[0]))
    slopes = jnp.exp2(-8.0 * (jnp.arange(N_HEADS, dtype=F32) + 1.0) / N_HEADS)
    lam_init = 0.8 - 0.6 * math.exp(-0.3 * 1)
    lq = lambda_qk[0].astype(F32)
    gsub = row(g_sub[0])

    def layer0(x, prev, pos0, tt, tm):
        b, t, _ = x.shape
        x1, state = _mixer(x, prev, wpool, row(pool_scale[0]), row(g_mix_pre[0]),
                           row(g_mix_post[0]), pos0=pos0, tt=tt)
        x2 = _ffn(x1.reshape(b * t, d), row(g_ffn_pre[0]), row(g_ffn_post[0]),
                  wg, wu, wd, layer=0, tm=tm)
        return x2, state

    def layer1_tail(x2, o, tm):
        return _ffn(x2, row(g_ffn_pre[1]), row(g_ffn_post[1]), wg, wu, wd, layer=1, tm=tm,
                    attn=(o, wo, row(g_mix_post[1])))

    zero_prev = jnp.zeros((batch, POOL_STATE_ROWS, d), F32)
    x2p, state_p = layer0(x_prompt, zero_prev, 0, TOKEN_TILE, TOKEN_TILE)
    kp, vp, _, kpb, vtpb, qpb = _proj(x2p, row(g_kv), row(g_mix_pre[1]), wk, wv, wq, tm=TOKEN_TILE)
    n_s = dec_batch * dec_seq
    x2s, state_s = layer0(x_sample, state_pool[0], past_len, dec_seq, n_s)
    ks, vs, qs, _, _, _ = _proj(x2s, row(g_kv), row(g_mix_pre[1]), wk, wv, wq, tm=n_s)

    op, os_ = _attention(qpb, kpb, vtpb, qs.reshape(dec_batch, dec_seq, -1),
                         ks.reshape(dec_batch, dec_seq, -1), vs.reshape(dec_batch, dec_seq, -1),
                         cache_k, cache_v, page_table, slopes, lq, gsub,
                         batch=batch, seq=seq, lam_init=lam_init)
    y_prompt = layer1_tail(x2p, op, TOKEN_TILE).reshape(batch, seq, d)
    y_sample = layer1_tail(x2s, os_, n_s).reshape(dec_batch, dec_seq, d)

    kv_shape_p = (batch, seq, N_HEADS, V_DIM)
    kv_shape_s = (dec_batch, dec_seq, N_HEADS, V_DIM)
    return (y_prompt, y_sample, state_p[None], state_s[None],
            kp.reshape(kv_shape_p), vp.reshape(kv_shape_p),
            ks.reshape(kv_shape_s), vs.reshape(kv_shape_s))
```

```python
import functools
import math

import jax
import jax.numpy as jnp
from jax import lax
from jax.experimental import pallas as pl
from jax.experimental.pallas import tpu as pltpu

F32 = jnp.float32
BF16 = jnp.bfloat16

D_MODEL = 1024
POOL_WINDOWS = (2, 4, 8, 16)
POOL_GROUP_DIM = D_MODEL // len(POOL_WINDOWS)
POOL_STATE_ROWS = max(POOL_WINDOWS) - 1
HALO_ROWS = POOL_STATE_ROWS + 1
HEAD_DIM = 64
V_DIM = 2 * HEAD_DIM
N_HEADS = D_MODEL // V_DIM
PAGE_SIZE = 128
PAGE_ROWS = PAGE_SIZE * N_HEADS
NORM_EPS = 1e-6
SUBLN_EPS = 1e-5
NEG_INF = -1e30
N_A_LAYERS = 1

VMEM_LIMIT_BYTES = 56 * 1024 * 1024
TOKEN_TILE = 512
ATTN_BLOCK = 256
PAGES_PER_STEP = 8
PAGE_RING = 3
N_SAMPLE_SCRATCH = 7
SAMPLE_MIXER_ROWS = 8
NEW_KEY_PAD = 16
LOG2E = math.log2(math.e)
POS_RADIX = 128
N_POS_FEATURES = 12


def _rms(xf, g, eps):
    return xf * lax.rsqrt(jnp.mean(xf * xf, axis=-1, keepdims=True) + eps) * g


def _const_spec(shape):
    zeros = (0,) * len(shape)
    return pl.BlockSpec(shape, lambda *_: zeros, pipeline_mode=pl.Buffered(1))


def _log2(n):
    assert n & (n - 1) == 0, n
    return n.bit_length() - 1


def _mixer_body(x_ref, prev_ref, wpool_ref, pscale_ref, gpre_ref, gpost_ref,
                out_ref, state_ref, ext_ref, y_ref, *, rows, tt, n_tiles, pos0):
    j = pl.program_id(1)
    pos = pos0 + j * tt + lax.broadcasted_iota(jnp.int32, (tt, 1), 0)
    for r in range(rows):
        ext = ext_ref.at[r]
        y = y_ref.at[r]

        @pl.when(j == 0)
        def _():
            ext[0:1, :] = jnp.zeros((1, D_MODEL), F32)
            ext[1:HALO_ROWS, :] = prev_ref[r]

        x = x_ref[r]
        h = _rms(x, gpre_ref[...], NORM_EPS)
        ext[HALO_ROWS:HALO_ROWS + tt, :] = h
        for g, w in enumerate(POOL_WINDOWS):
            cols = slice(g * POOL_GROUP_DIM, (g + 1) * POOL_GROUP_DIM)
            e = h[:, cols]
            s = e
            for k in range(1, w):
                s = s + ext[HALO_ROWS - k:HALO_ROWS - k + tt, cols]
            inv_cnt = 1.0 / jnp.minimum(w, pos + 1).astype(F32)
            diff = (s * inv_cnt - e).astype(BF16)
            yg = jnp.dot(diff, wpool_ref[g], preferred_element_type=F32)
            y[:, cols] = yg * pscale_ref[:, cols]
        out_ref[r] = x + _rms(y[...], gpost_ref[...], NORM_EPS)

        @pl.when(j == n_tiles - 1)
        def _():
            state_ref[r] = ext[tt + 1:tt + HALO_ROWS, :]

        if n_tiles > 1:
            ext[0:HALO_ROWS, :] = ext[tt:tt + HALO_ROWS, :]


def _mixer(x, prev, wpool, pscale, gpre, gpost, *, pos0, tt, rows=1):
    b, t, d = x.shape
    n_tiles = t // tt
    body = functools.partial(_mixer_body, rows=rows, tt=tt, n_tiles=n_tiles, pos0=pos0)
    return pl.pallas_call(
        body,
        grid=(b // rows, n_tiles),
        in_specs=[
            pl.BlockSpec((rows, tt, d), lambda i, j: (i, j, 0)),
            pl.BlockSpec((rows, POOL_STATE_ROWS, d), lambda i, j: (i, 0, 0)),
            _const_spec(wpool.shape),
            _const_spec((1, d)),
            _const_spec((1, d)),
            _const_spec((1, d)),
        ],
        out_specs=[
            pl.BlockSpec((rows, tt, d), lambda i, j: (i, j, 0)),
            pl.BlockSpec((rows, POOL_STATE_ROWS, d), lambda i, j: (i, 0, 0)),
        ],
        out_shape=[
            jax.ShapeDtypeStruct((b, t, d), F32),
            jax.ShapeDtypeStruct((b, POOL_STATE_ROWS, d), F32),
        ],
        scratch_shapes=[
            pltpu.VMEM((rows, HALO_ROWS + tt, d), F32),
            pltpu.VMEM((rows, tt, d), F32),
        ],
        compiler_params=pltpu.CompilerParams(
            dimension_semantics=("arbitrary", "arbitrary"),
            vmem_limit_bytes=VMEM_LIMIT_BYTES),
        name="pool_mixer",
    )(x, prev, wpool, pscale, gpre, gpost)


def _ffn_body(*refs, with_attn, ff_chunks):
    if with_attn:
        (x_ref, o_ref, wo_ref, gmix_ref, gpre_ref, gpost_ref, wg_ref, wu_ref, wd_ref,
         out_ref, act_ref) = refs
    else:
        x_ref, gpre_ref, gpost_ref, wg_ref, wu_ref, wd_ref, out_ref, act_ref = refs
    x = x_ref[...]
    if with_attn:
        y = jnp.dot(o_ref[...].astype(BF16), wo_ref[...], preferred_element_type=F32)
        x = x + _rms(y, gmix_ref[...], NORM_EPS)
    h = _rms(x, gpre_ref[...], NORM_EPS).astype(BF16)
    for c0, c1 in ff_chunks:
        gate = jnp.dot(h, wg_ref[:, c0:c1], preferred_element_type=F32)
        up = jnp.dot(h, wu_ref[:, c0:c1], preferred_element_type=F32)
        act_ref[:, c0:c1] = (gate * jax.nn.sigmoid(gate) * up).astype(BF16)
    f = jnp.dot(act_ref[...], wd_ref[...], preferred_element_type=F32)
    out_ref[...] = x + _rms(f, gpost_ref[...], NORM_EPS)


def _layer_spec(stack, layer):
    return pl.BlockSpec((None,) + stack.shape[1:], lambda *_: (layer, 0, 0),
                        pipeline_mode=pl.Buffered(1))


def _ffn(x, gpre, gpost, wg, wu, wd, *, layer, tm, attn=None):
    n, d = x.shape
    d_ff = wg.shape[2]
    half = d_ff // 2
    ff_chunks = ((0, half), (half, d_ff)) if half % 128 == 0 else ((0, d_ff),)
    row_spec = pl.BlockSpec((tm, d), lambda i: (i, 0))
    args = [x]
    in_specs = [row_spec]
    if attn is not None:
        o, wo, gmix = attn
        args += [o, wo, gmix]
        in_specs += [pl.BlockSpec((tm, o.shape[1]), lambda i: (i, 0)),
                     _const_spec(wo.shape), _const_spec((1, d))]
    args += [gpre, gpost, wg, wu, wd]
    in_specs += [_const_spec((1, d)), _const_spec((1, d)),
                 _layer_spec(wg, layer), _layer_spec(wu, layer), _layer_spec(wd, layer)]
    body = functools.partial(_ffn_body, with_attn=attn is not None, ff_chunks=ff_chunks)
    return pl.pallas_call(
        body,
        grid=(n // tm,),
        in_specs=in_specs,
        out_specs=row_spec,
        out_shape=jax.ShapeDtypeStruct((n, d), F32),
        scratch_shapes=[pltpu.VMEM((tm, d_ff), BF16)],
        compiler_params=pltpu.CompilerParams(
            dimension_semantics=("arbitrary",),
            vmem_limit_bytes=VMEM_LIMIT_BYTES),
        name="attn_out_ffn" if attn is not None else "ffn",
    )(*args)


def _proj_body(x_ref, gkv_ref, gq_ref, wk_ref, wv_ref, wq_ref,
               k_ref, v_ref, q_ref, kb_ref, vtb_ref, qb_ref):
    x = x_ref[...]
    src = _rms(x, gkv_ref[...], NORM_EPS).astype(BF16)
    k = jnp.dot(src, wk_ref[...], preferred_element_type=F32)
    k_ref[...] = k
    kb_ref[...] = k.astype(BF16)
    v = jnp.dot(src, wv_ref[...], preferred_element_type=F32)
    v_ref[...] = v
    vtb_ref[...] = v.T.astype(BF16)
    h = _rms(x, gq_ref[...], NORM_EPS).astype(BF16)
    q = jnp.dot(h, wq_ref[...], preferred_element_type=F32) * (HEAD_DIM ** -0.5)
    q_ref[...] = q
    qb_ref[...] = (q * LOG2E).astype(BF16)


def _proj(x, gkv, gq, wk, wv, wq, *, tm):
    n, d = x.shape
    w = wk.shape[1]
    row_in = pl.BlockSpec((tm, d), lambda i: (i, 0))
    row_out = pl.BlockSpec((tm, w), lambda i: (i, 0))
    col_out = pl.BlockSpec((w, tm), lambda i: (0, i))
    row_f32, row_bf16 = jax.ShapeDtypeStruct((n, w), F32), jax.ShapeDtypeStruct((n, w), BF16)
    return pl.pallas_call(
        _proj_body,
        grid=(n // tm,),
        in_specs=[row_in, _const_spec((1, d)), _const_spec((1, d)),
                  _const_spec(wk.shape), _const_spec(wv.shape), _const_spec(wq.shape)],
        out_specs=[row_out, row_out, row_out, row_out, col_out, row_out],
        out_shape=[row_f32, row_f32, row_f32, row_bf16,
                   jax.ShapeDtypeStruct((w, n), BF16), row_bf16],
        compiler_params=pltpu.CompilerParams(
            dimension_semantics=("arbitrary",),
            vmem_limit_bytes=VMEM_LIMIT_BYTES),
        name="kvq_proj",
    )(x, gkv, gq, wk, wv, wq)


def _lambda(lq_ref, lam_init):
    lq = lq_ref[...]
    a = jnp.sum(lq[0:1] * lq[1:2], axis=-1, keepdims=True)
    b = jnp.sum(lq[2:3] * lq[3:4], axis=-1, keepdims=True)
    return jnp.exp(a) - jnp.exp(b) + lam_init


def _diff_finalize(o1, o2, lq_ref, gsub_ref, lam_init):
    od = o1 - _lambda(lq_ref, lam_init) * o2
    return _rms(od, gsub_ref[...], SUBLN_EPS) * (1.0 - lam_init)


def _online_update(m_ref, l_ref, acc_ref, m_cur):
    m_prev = m_ref[...]
    m_new = jnp.maximum(m_prev, m_cur)
    alpha = jnp.exp(m_prev - m_new)
    m_ref[...] = m_new
    l_ref[...] = alpha * l_ref[...]
    acc_ref[...] = alpha * acc_ref[...]
    return m_new


def _alibi_feature_rows(slopes):
    cs = slopes * LOG2E
    c1 = cs.astype(BF16).astype(F32)
    c2 = (cs - c1).astype(BF16).astype(F32)
    c3 = (cs - c1 - c2).astype(BF16).astype(F32)
    c = jnp.stack([c1, c2, c3], axis=1)
    half = N_POS_FEATURES // 2
    pad = jnp.zeros((slopes.shape[0], V_DIM - N_POS_FEATURES), F32)
    zeros = jnp.zeros((slopes.shape[0], half), F32)
    q_rows = jnp.concatenate([POS_RADIX * c, c, zeros, pad], axis=1)
    k_rows = jnp.concatenate([zeros, -POS_RADIX * c, -c, pad], axis=1)
    return q_rows, k_rows


def _prompt_phases(item, q_ref, k_ref, vt_ref, qrow_ref, krow_ref, lq_ref, gsub_ref, o_ref,
                   pos_ref, mask_ref, kaug_ref, qfeat_ref, s_ref, p_ref, *, blk, lam_init):
    seq = q_ref.shape[0]
    nq = seq // blk
    slot = item % nq
    head = (item // nq) % N_HEADS
    half = N_POS_FEATURES // 2
    contract_last = (((1,), (1,)), ((), ()))

    def init_tables():
        pos = lax.broadcasted_iota(jnp.int32, (seq, V_DIM), 0)
        lane = lax.broadcasted_iota(jnp.int32, (seq, V_DIM), 1)
        hi = (pos >> _log2(POS_RADIX)).astype(F32)
        lo = (pos & (POS_RADIX - 1)).astype(F32)
        third = half // 2
        is_hi = (lane < third) | ((lane >= half) & (lane < half + third))
        pos_ref[...] = jnp.where(lane >= N_POS_FEATURES, 0.0, jnp.where(is_hi, hi, lo))
        key = lax.broadcasted_iota(jnp.int32, (blk, 2 * blk), 0)
        qry = lax.broadcasted_iota(jnp.int32, (blk, 2 * blk), 1) & (blk - 1)
        mask_ref[...] = jnp.where(key <= qry, 0.0, NEG_INF)

    def prepare_head():
        lane = lax.broadcasted_iota(jnp.int32, (seq, V_DIM), 1)
        pos_feat = pos_ref[...]
        kaug_ref[:, 0:V_DIM] = k_ref[...]
        kaug_ref[:, V_DIM:] = jnp.where(lane < half, pos_feat,
                                        krow_ref[pl.ds(head, 1), :]).astype(BF16)
        qfeat_ref[...] = jnp.where(lane < half, qrow_ref[pl.ds(head, 1), :], pos_feat).astype(BF16)

    def query_block(i):
        lane = lax.broadcasted_iota(jnp.int32, (blk, V_DIM), 1)
        n_keys = (i + 1) * blk
        q = q_ref[i * blk:(i + 1) * blk, :]
        zero = jnp.zeros_like(q)
        qf = qfeat_ref[i * blk:(i + 1) * blk, :]
        q2 = jnp.concatenate(
            [jnp.concatenate([jnp.where(lane < HEAD_DIM, q, zero), qf], axis=1),
             jnp.concatenate([jnp.where(lane >= HEAD_DIM, q, zero), qf], axis=1)], axis=0)
        s_ref[0:n_keys, :] = lax.dot_general(kaug_ref[0:n_keys, :], q2, contract_last,
                                             preferred_element_type=F32)
        diag = slice(i * blk, (i + 1) * blk)
        s_ref[diag, :] = s_ref[diag, :] + mask_ref[...]
        m = None
        for j in range(i + 1):
            m_j = jnp.max(s_ref[j * blk:(j + 1) * blk, :], axis=0, keepdims=True)
            m = m_j if m is None else jnp.maximum(m, m_j)
        l = jnp.zeros_like(m)
        for j in range(i + 1):
            rows = slice(j * blk, (j + 1) * blk)
            p = jnp.exp2(s_ref[rows, :] - m)
            l = l + jnp.sum(p, axis=0, keepdims=True)
            p_ref[rows, :] = p.astype(BF16)
        ot = jnp.dot(vt_ref[:, 0:n_keys], p_ref[0:n_keys, :], preferred_element_type=F32) / l
        o = _diff_finalize(ot[:, :blk].T, ot[:, blk:].T, lq_ref, gsub_ref, lam_init)
        o_ref[i * blk:(i + 1) * blk, :] = o.astype(o_ref.dtype)

    def prepare():
        pl.when(item == 0)(init_tables)
        pl.when(slot == 0)(prepare_head)

    return prepare, slot, [functools.partial(query_block, i) for i in range(nq)]


def _prompt_scratch(seq, blk):
    return [
        pltpu.VMEM((seq, V_DIM), F32),
        pltpu.VMEM((blk, 2 * blk), F32),
        pltpu.VMEM((seq, 2 * V_DIM), BF16),
        pltpu.VMEM((seq, V_DIM), BF16),
        pltpu.VMEM((seq, 2 * blk), F32),
        pltpu.VMEM((seq, 2 * blk), BF16),
    ]


def _sample_phases(g, q_ref, qprev_ref, slope_ref, k_refs, v_refs, knew_ref, vnew_ref, lq_ref,
                   gsub_ref, o_ref, bias_ref, p_ref, mp_ref, lp_ref, m_ref, l_ref, acc_ref, *,
                   n_steps, n_groups, past_len, dec_seq, lam_init):
    pages_per_step = len(k_refs)
    prev_step = _value_group(g) % n_steps
    n_rows = q_ref.shape[1]
    slope = slope_ref[:, 0:1]
    contract_last = (((1,), (1,)), ((), ()))

    def row_col_ids(shape):
        r = lax.broadcasted_iota(jnp.int32, shape, 0)
        c = lax.broadcasted_iota(jnp.int32, shape, 1)
        head_q = (r >> _log2(dec_seq)) & (N_HEADS - 1)
        return head_q, r & (dec_seq - 1), c >> _log2(N_HEADS), c & (N_HEADS - 1)

    def init_tables():
        hq, tok, key, hk = row_col_ids((n_rows, PAGE_ROWS))
        bias_ref[...] = jnp.where(hq == hk, -slope * (tok - key).astype(F32), NEG_INF)
        p_ref[...] = jnp.zeros(p_ref.shape, p_ref.dtype)
        mp_ref[...] = jnp.full(mp_ref.shape, NEG_INF, F32)
        lp_ref[...] = jnp.zeros(lp_ref.shape, F32)

    def reset_row():
        m_ref[...] = jnp.full(m_ref.shape, NEG_INF, F32)
        l_ref[...] = jnp.zeros(l_ref.shape, F32)
        acc_ref[...] = jnp.zeros(acc_ref.shape, F32)

    def prepare():
        pl.when(g == 0)(init_tables)
        pl.when(prev_step == 0)(reset_row)

    def body():
        m_cur = mp_ref[0]
        for i in range(1, pages_per_step):
            m_cur = jnp.maximum(m_cur, mp_ref[i])
        m_new = _online_update(m_ref, l_ref, acc_ref, m_cur)
        l_add = jnp.zeros(l_ref.shape, F32)
        acc_add = jnp.zeros(acc_ref.shape, F32)
        for i in range(pages_per_step):
            w = jnp.exp(mp_ref[i] - m_new)
            l_add = l_add + w * lp_ref[i]
            acc_add = acc_add + w * jnp.dot(p_ref[i], v_refs[i][...], preferred_element_type=F32)
        l_ref[...] += l_add
        acc_ref[...] += acc_add

        q = q_ref[0]
        bias = bias_ref[...]
        step = _score_group(g, n_groups) % n_steps
        for i in range(pages_per_step):
            s = lax.dot_general(q, k_refs[i][...], contract_last, preferred_element_type=F32)
            t = s + bias
            m_page = jnp.max(t, axis=-1, keepdims=True)
            p = jnp.exp(t - m_page)
            p_ref[i] = p.astype(p_ref.dtype)
            lp_ref[i] = jnp.sum(p, axis=-1, keepdims=True)
            page_pos = (step * pages_per_step + i) * PAGE_SIZE
            mp_ref[i] = m_page - slope * (past_len - page_pos).astype(F32)

    def finish_row():
        qp = qprev_ref[0]
        hq, tok, key, hk = row_col_ids((n_rows, knew_ref.shape[1]))
        valid = (hq == hk) & (key <= tok)
        bias_new = jnp.where(valid, -slope * (tok - key).astype(F32), NEG_INF)
        s = lax.dot_general(qp, knew_ref[0], contract_last, preferred_element_type=F32)
        t = s + bias_new
        m_fin = _online_update(m_ref, l_ref, acc_ref, jnp.max(t, axis=-1, keepdims=True))
        p = jnp.exp(t - m_fin)
        l = l_ref[...] + jnp.sum(p, axis=-1, keepdims=True)
        acc = acc_ref[...] + jnp.dot(p, vnew_ref[0], preferred_element_type=F32)
        o = acc / l
        half = n_rows // 2
        o_ref[0] = _diff_finalize(o[:half], o[half:], lq_ref, gsub_ref, lam_init)

    def finish():
        pl.when((g > 0) & (prev_step == n_steps - 1))(finish_row)

    return prepare, body, finish


def _score_group(g, n_groups):
    return jnp.minimum(g, n_groups - 1)


def _value_group(g):
    return jnp.maximum(g - 1, 0)


def _attention_body(pt_ref, q_ref, qprev_ref, slope_ref, ck_ref, cv_ref, knew_ref, vnew_ref, lq_ref,
                    gsub_ref, qb_ref, kb_ref, vtb_ref, qrow_ref, krow_ref, os_ref, op_ref,
                    kbuf_ref, vbuf_ref, ksem, vsem, *scratch, pages_per_step, n_groups, n_items,
                    blk, sample_args, lam_init):
    sample_scratch = scratch[:N_SAMPLE_SCRATCH]
    prompt_scratch = scratch[N_SAMPLE_SCRATCH:]
    g = pl.program_id(0)

    def page_copies(cache_ref, buf_ref, sem, group):
        slot = group % PAGE_RING
        copies = []
        for i in range(pages_per_step):
            row0 = pl.multiple_of(pt_ref[group * pages_per_step + i] * PAGE_ROWS, PAGE_ROWS)
            copies.append(pltpu.make_async_copy(
                cache_ref.at[pl.ds(row0, PAGE_ROWS), :], buf_ref.at[slot, i], sem.at[slot]))
        return copies

    def start_keys(group):
        for i, copy in enumerate(page_copies(ck_ref, kbuf_ref, ksem, group)):
            copy.start(priority=i % 2)

    def start_values(group):
        for i, copy in enumerate(page_copies(cv_ref, vbuf_ref, vsem, group)):
            copy.start(priority=i % 2)

    @pl.when(g == 0)
    def _():
        start_keys(0)
        start_values(0)
        if n_groups > 1:
            start_keys(1)
        vbuf_ref[PAGE_RING - 1] = jnp.zeros(vbuf_ref.shape[1:], vbuf_ref.dtype)

    pl.when(g + 2 < n_groups)(lambda: start_keys(g + 2))
    pl.when(g + 1 < n_groups)(lambda: start_values(g + 1))

    @pl.when(g < n_groups)
    def _():
        for copy in page_copies(ck_ref, kbuf_ref, ksem, g):
            copy.wait()

    @pl.when(g > 0)
    def _():
        for copy in page_copies(cv_ref, vbuf_ref, vsem, g - 1):
            copy.wait()

    key_slot = g % PAGE_RING
    value_slot = (g + PAGE_RING - 1) % PAGE_RING
    k_refs = [kbuf_ref.at[key_slot, i] for i in range(pages_per_step)]
    v_refs = [vbuf_ref.at[value_slot, i] for i in range(pages_per_step)]
    sample_prepare, sample_body, sample_finish = _sample_phases(
        g, q_ref, qprev_ref, slope_ref, k_refs, v_refs, knew_ref, vnew_ref, lq_ref, gsub_ref,
        os_ref, *sample_scratch, lam_init=lam_init, **sample_args)
    has_item = g < n_items
    prompt_prepare, slot, slot_bodies = _prompt_phases(
        jnp.minimum(g, n_items - 1), qb_ref, kb_ref, vtb_ref, qrow_ref, krow_ref, lq_ref,
        gsub_ref, op_ref, *prompt_scratch, blk=blk, lam_init=lam_init)

    sample_prepare()
    pl.when(has_item)(prompt_prepare)

    def both(slot_body):
        sample_body()
        slot_body()

    for j, slot_body in enumerate(slot_bodies):
        pl.when(has_item & (slot == j))(functools.partial(both, slot_body))
    pl.when(jnp.logical_not(has_item))(sample_body)
    sample_finish()


def _attention(qb, kb, vtb, q_s, k_s, v_s, cache_k, cache_v, page_table, slopes, lq, gsub, *,
               batch, seq, lam_init):
    dec_batch, dec_seq, _ = q_s.shape
    n_pages = page_table.shape[1]
    pps = PAGES_PER_STEP
    n_steps = n_pages // pps
    n_groups = dec_batch * n_steps
    n_rows = 2 * N_HEADS * dec_seq
    blk = ATTN_BLOCK
    n_slots = seq // blk
    n_items = batch * N_HEADS * n_slots
    assert n_items <= n_groups + 1, (n_items, n_groups)

    q5 = q_s.reshape(dec_batch, dec_seq, N_HEADS, 2, HEAD_DIM)
    zeros = jnp.zeros_like(q5[..., 0, :])
    qm = jnp.stack([jnp.concatenate([q5[..., 0, :], zeros], axis=-1),
                    jnp.concatenate([zeros, q5[..., 1, :]], axis=-1)], axis=1)
    qall = qm.transpose(0, 1, 3, 2, 4).reshape(dec_batch, n_rows, V_DIM)
    row_slope = jnp.broadcast_to(jnp.tile(jnp.repeat(slopes, dec_seq), 2)[:, None], (n_rows, 128))

    def pad_new(a):
        a = a.reshape(dec_batch, dec_seq, N_HEADS, V_DIM)
        a = jnp.pad(a, ((0, 0), (0, NEW_KEY_PAD - dec_seq), (0, 0), (0, 0)))
        return a.reshape(dec_batch, NEW_KEY_PAD * N_HEADS, V_DIM)

    knew, vnew = pad_new(k_s), pad_new(v_s)
    ck = cache_k.reshape(-1, V_DIM)
    cv = cache_v.reshape(-1, V_DIM)
    q_rows, k_rows = _alibi_feature_rows(slopes)

    score_row = lambda g, pt: (_score_group(g, n_groups) // n_steps, 0, 0)
    value_row = lambda g, pt: (_value_group(g) // n_steps, 0, 0)
    const2 = lambda g, pt: (0, 0)

    def pair(g):
        p = jnp.minimum(g, n_items - 1) // n_slots
        return p // N_HEADS, p % N_HEADS

    head_rows = pl.BlockSpec((seq, V_DIM), lambda g, pt: pair(g))
    head_cols = pl.BlockSpec((V_DIM, seq), lambda g, pt: pair(g)[::-1])
    cache_spec = pl.BlockSpec(memory_space=pl.ANY)
    page_ring = pltpu.VMEM((PAGE_RING, pps, PAGE_ROWS, V_DIM), cache_k.dtype)
    sample_scratch = [
        pltpu.VMEM((n_rows, PAGE_ROWS), F32),
        pltpu.VMEM((pps, n_rows, PAGE_ROWS), BF16),
        pltpu.VMEM((pps, n_rows, 1), F32),
        pltpu.VMEM((pps, n_rows, 1), F32),
        pltpu.VMEM((n_rows, 1), F32),
        pltpu.VMEM((n_rows, 1), F32),
        pltpu.VMEM((n_rows, V_DIM), F32),
    ]
    assert len(sample_scratch) == N_SAMPLE_SCRATCH
    body = functools.partial(
        _attention_body, pages_per_step=pps, n_groups=n_groups, n_items=n_items, blk=blk,
        lam_init=lam_init,
        sample_args=dict(n_steps=n_steps, n_groups=n_groups, past_len=n_pages * PAGE_SIZE,
                         dec_seq=dec_seq))
    grid_spec = pltpu.PrefetchScalarGridSpec(
        num_scalar_prefetch=1,
        grid=(n_groups + 1,),
        in_specs=[
            pl.BlockSpec((1, n_rows, V_DIM), score_row),
            pl.BlockSpec((1, n_rows, V_DIM), value_row),
            pl.BlockSpec(row_slope.shape, const2),
            cache_spec, cache_spec,
            pl.BlockSpec((1,) + knew.shape[1:], value_row),
            pl.BlockSpec((1,) + vnew.shape[1:], value_row),
            pl.BlockSpec(lq.shape, const2),
            pl.BlockSpec(gsub.shape, const2),
            head_rows, head_rows, head_cols,
            pl.BlockSpec(q_rows.shape, const2),
            pl.BlockSpec(k_rows.shape, const2),
        ],
        out_specs=[pl.BlockSpec((1, n_rows // 2, V_DIM), value_row), head_rows],
        scratch_shapes=[
            page_ring, page_ring,
            pltpu.SemaphoreType.DMA((PAGE_RING,)), pltpu.SemaphoreType.DMA((PAGE_RING,)),
            *sample_scratch, *_prompt_scratch(seq, blk),
        ],
    )
    o_s, o_p = pl.pallas_call(
        body,
        grid_spec=grid_spec,
        out_shape=[jax.ShapeDtypeStruct((dec_batch, n_rows // 2, V_DIM), F32),
                   jax.ShapeDtypeStruct(qb.shape, BF16)],
        compiler_params=pltpu.CompilerParams(
            dimension_semantics=("arbitrary",),
            vmem_limit_bytes=VMEM_LIMIT_BYTES),
        name="diff_attention",
    )(page_table.reshape(-1), qall, qall, row_slope, ck, cv,
      knew, vnew, lq, gsub, qb, kb, vtb, q_rows, k_rows)
    o_s = o_s.reshape(dec_batch, N_HEADS, dec_seq, V_DIM).transpose(0, 2, 1, 3)
    return o_p, o_s.reshape(dec_batch * dec_seq, N_HEADS * V_DIM)


def kernel(x_prompt, x_sample, state_pool, cache_k, cache_v, page_table, w_pool, pool_scale, g_kv,
           w_k, w_v, w_q, w_o, lambda_qk, g_sub, g_mix_pre, g_mix_post, g_ffn_pre, g_ffn_post,
           w_gate, w_up, w_down):
    batch, seq, d = x_prompt.shape
    dec_batch, dec_seq, _ = x_sample.shape
    past_len = page_table.shape[1] * PAGE_SIZE
    depth = g_mix_pre.shape[0]
    assert depth == 2 and w_pool.shape[0] == N_A_LAYERS and d == D_MODEL

    row = lambda a: a.reshape(1, -1)
    wpool = w_pool[0].astype(BF16)
    wg, wu, wd = (w.astype(BF16) for w in (w_gate, w_up, w_down))
    wk, wv, wq, wo = (w.astype(BF16) for w in (w_k, w_v, w_q[0], w_o[0]))
    slopes = jnp.exp2(-8.0 * (jnp.arange(N_HEADS, dtype=F32) + 1.0) / N_HEADS)
    lam_init = 0.8 - 0.6 * math.exp(-0.3 * 1)
    lq = lambda_qk[0].astype(F32)
    gsub = row(g_sub[0])

    def layer0(x, prev, pos0, tt, tm, rows=1):
        b, t, _ = x.shape
        x1, state = _mixer(x, prev, wpool, row(pool_scale[0]), row(g_mix_pre[0]),
                           row(g_mix_post[0]), pos0=pos0, tt=tt, rows=rows)
        x2 = _ffn(x1.reshape(b * t, d), row(g_ffn_pre[0]), row(g_ffn_post[0]),
                  wg, wu, wd, layer=0, tm=tm)
        return x2, state

    def layer1_tail(x2, o, tm):
        return _ffn(x2, row(g_ffn_pre[1]), row(g_ffn_post[1]), wg, wu, wd, layer=1, tm=tm,
                    attn=(o, wo, row(g_mix_post[1])))

    zero_prev = jnp.zeros((batch, POOL_STATE_ROWS, d), F32)
    x2p, state_p = layer0(x_prompt, zero_prev, 0, TOKEN_TILE, TOKEN_TILE)
    kp, vp, _, kpb, vtpb, qpb = _proj(x2p, row(g_kv), row(g_mix_pre[1]), wk, wv, wq, tm=TOKEN_TILE)
    n_s = dec_batch * dec_seq
    x2s, state_s = layer0(x_sample, state_pool[0], past_len, dec_seq, n_s, rows=SAMPLE_MIXER_ROWS)
    ks, vs, qs, _, _, _ = _proj(x2s, row(g_kv), row(g_mix_pre[1]), wk, wv, wq, tm=n_s)

    op, os_ = _attention(qpb, kpb, vtpb, qs.reshape(dec_batch, dec_seq, -1),
                         ks.reshape(dec_batch, dec_seq, -1), vs.reshape(dec_batch, dec_seq, -1),
                         cache_k, cache_v, page_table, slopes, lq, gsub,
                         batch=batch, seq=seq, lam_init=lam_init)
    y_prompt = layer1_tail(x2p, op, TOKEN_TILE).reshape(batch, seq, d)
    y_sample = layer1_tail(x2s, os_, n_s).reshape(dec_batch, dec_seq, d)

    kv_shape_p = (batch, seq, N_HEADS, V_DIM)
    kv_shape_s = (dec_batch, dec_seq, N_HEADS, V_DIM)
    return (y_prompt, y_sample, state_p[None], state_s[None],
            kp.reshape(kv_shape_p), vp.reshape(kv_shape_p),
            ks.reshape(kv_shape_s), vs.reshape(kv_shape_s))
```

```python
import functools
import math

import jax
import jax.numpy as jnp
from jax import lax
from jax.experimental import pallas as pl
from jax.experimental.pallas import tpu as pltpu

F32 = jnp.float32
BF16 = jnp.bfloat16

D_MODEL = 1024
POOL_WINDOWS = (2, 4, 8, 16)
POOL_GROUP_DIM = D_MODEL // len(POOL_WINDOWS)
POOL_STATE_ROWS = max(POOL_WINDOWS) - 1
HALO_ROWS = POOL_STATE_ROWS + 1
HEAD_DIM = 64
V_DIM = 2 * HEAD_DIM
N_HEADS = D_MODEL // V_DIM
PAGE_SIZE = 128
PAGE_ROWS = PAGE_SIZE * N_HEADS
NORM_EPS = 1e-6
SUBLN_EPS = 1e-5
NEG_INF = -1e30
N_A_LAYERS = 1

VMEM_LIMIT_BYTES = 56 * 1024 * 1024
TOKEN_TILE = 512
ATTN_BLOCK = 256
PAGES_PER_STEP = 8
PAGE_RING = 3
N_SAMPLE_SCRATCH = 7
SAMPLE_MIXER_ROWS = 8
NEW_KEY_PAD = 16
LOG2E = math.log2(math.e)
POS_RADIX = 128
N_POS_FEATURES = 12


def _rms(xf, g, eps):
    return xf * lax.rsqrt(jnp.mean(xf * xf, axis=-1, keepdims=True) + eps) * g


def _const_spec(shape):
    zeros = (0,) * len(shape)
    return pl.BlockSpec(shape, lambda *_: zeros, pipeline_mode=pl.Buffered(1))


def _log2(n):
    assert n & (n - 1) == 0, n
    return n.bit_length() - 1


def _mixer_load_prev(prev_ref, ext, j):
    @pl.when(j == 0)
    def _():
        ext[0:1, :] = jnp.zeros((1, D_MODEL), F32)
        ext[1:HALO_ROWS, :] = prev_ref[...]


def _mixer_tile(x_ref, prev_ref, state_ref, ext, y, wpool_ref, pscale_ref, gpre_ref, gpost_ref, *,
                j, pos, tt, n_tiles, live=True, load_prev=True):
    if load_prev:
        _mixer_load_prev(prev_ref, ext, j)
    stages, result = _mixer_stages(x_ref, state_ref, ext, y, wpool_ref, pscale_ref, gpre_ref,
                                   gpost_ref, j=j, pos=pos, tt=tt, n_tiles=n_tiles, live=live)
    for stage in stages:
        stage()
    return result["x1"], result["store_state"]


def _mixer_stages(x_ref, state_ref, ext, y, wpool_ref, pscale_ref, gpre_ref, gpost_ref, *,
                  j, pos, tt, n_tiles, live):
    vals = {}

    def norm():
        vals["x"] = x_ref[...]
        vals["h"] = _rms(vals["x"], gpre_ref[...], NORM_EPS)
        ext[HALO_ROWS:HALO_ROWS + tt, :] = vals["h"]

    def group(g, w):
        cols = slice(g * POOL_GROUP_DIM, (g + 1) * POOL_GROUP_DIM)
        e = vals["h"][:, cols]
        s = e
        for k in range(1, w):
            s = s + ext[HALO_ROWS - k:HALO_ROWS - k + tt, cols]
        inv_cnt = 1.0 / jnp.minimum(w, pos + 1).astype(F32)
        diff = (s * inv_cnt - e).astype(BF16)
        yg = jnp.dot(diff, wpool_ref[g], preferred_element_type=F32)
        y[:, cols] = yg * pscale_ref[:, cols]

    def output():
        vals["x1"] = vals["x"] + _rms(y[...], gpost_ref[...], NORM_EPS)
        assert n_tiles == 1 or tt >= HALO_ROWS
        state_rows = ext[tt + 1:tt + HALO_ROWS, :]
        if n_tiles > 1:
            ext[0:HALO_ROWS, :] = ext[tt:tt + HALO_ROWS, :]

        def store_state():
            @pl.when(live & (j == n_tiles - 1))
            def _():
                state_ref[...] = state_rows

        vals["store_state"] = store_state

    stages = [norm] + [functools.partial(group, g, w) for g, w in enumerate(POOL_WINDOWS)]
    return stages + [output], vals


def _mixer_body(x_ref, prev_ref, wpool_ref, pscale_ref, gpre_ref, gpost_ref,
                out_ref, state_ref, ext_ref, y_ref, *, rows, tt, n_tiles, pos0):
    j = pl.program_id(1)
    pos = pos0 + j * tt + lax.broadcasted_iota(jnp.int32, (tt, 1), 0)
    for r in range(rows):
        x1, store_state = _mixer_tile(
            x_ref.at[r], prev_ref.at[r], state_ref.at[r], ext_ref.at[r], y_ref.at[r], wpool_ref,
            pscale_ref, gpre_ref, gpost_ref, j=j, pos=pos, tt=tt, n_tiles=n_tiles)
        out_ref[r] = x1
        store_state()


def _mixer(x, prev, wpool, pscale, gpre, gpost, *, pos0, tt, rows=1):
    b, t, d = x.shape
    n_tiles = t // tt
    body = functools.partial(_mixer_body, rows=rows, tt=tt, n_tiles=n_tiles, pos0=pos0)
    return pl.pallas_call(
        body,
        grid=(b // rows, n_tiles),
        in_specs=[
            pl.BlockSpec((rows, tt, d), lambda i, j: (i, j, 0)),
            pl.BlockSpec((rows, POOL_STATE_ROWS, d), lambda i, j: (i, 0, 0)),
            _const_spec(wpool.shape),
            _const_spec((1, d)),
            _const_spec((1, d)),
            _const_spec((1, d)),
        ],
        out_specs=[
            pl.BlockSpec((rows, tt, d), lambda i, j: (i, j, 0)),
            pl.BlockSpec((rows, POOL_STATE_ROWS, d), lambda i, j: (i, 0, 0)),
        ],
        out_shape=[
            jax.ShapeDtypeStruct((b, t, d), F32),
            jax.ShapeDtypeStruct((b, POOL_STATE_ROWS, d), F32),
        ],
        scratch_shapes=[
            pltpu.VMEM((rows, HALO_ROWS + tt, d), F32),
            pltpu.VMEM((rows, tt, d), F32),
        ],
        compiler_params=pltpu.CompilerParams(
            dimension_semantics=("arbitrary", "arbitrary"),
            vmem_limit_bytes=VMEM_LIMIT_BYTES),
        name="pool_mixer",
    )(x, prev, wpool, pscale, gpre, gpost)


def _ffn_body(*refs, with_attn, ff_chunks):
    if with_attn:
        (x_ref, o_ref, wo_ref, gmix_ref, gpre_ref, gpost_ref, wg_ref, wu_ref, wd_ref,
         out_ref, act_ref) = refs
    else:
        x_ref, gpre_ref, gpost_ref, wg_ref, wu_ref, wd_ref, out_ref, act_ref = refs
    x = x_ref[...]
    if with_attn:
        y = jnp.dot(o_ref[...].astype(BF16), wo_ref[...], preferred_element_type=F32)
        x = x + _rms(y, gmix_ref[...], NORM_EPS)
    out_ref[...] = _ffn_tile(x, gpre_ref, gpost_ref, wg_ref, wu_ref, wd_ref, act_ref, ff_chunks)


def _ffn_tile(x, gpre_ref, gpost_ref, wg_ref, wu_ref, wd_ref, act_ref, ff_chunks):
    stages, result = _ffn_stages(lambda: x, gpre_ref, gpost_ref, wg_ref, wu_ref, wd_ref, act_ref,
                                 ff_chunks)
    for stage in stages:
        stage()
    return result["out"]


def _ffn_stages(load_x, gpre_ref, gpost_ref, wg_ref, wu_ref, wd_ref, act_ref, ff_chunks):
    vals = {}

    def norm():
        vals["x"] = load_x()
        vals["h"] = _rms(vals["x"], gpre_ref[...], NORM_EPS).astype(BF16)

    def chunk(c0, c1):
        gate = jnp.dot(vals["h"], wg_ref[:, c0:c1], preferred_element_type=F32)
        up = jnp.dot(vals["h"], wu_ref[:, c0:c1], preferred_element_type=F32)
        act_ref[:, c0:c1] = (gate * jax.nn.sigmoid(gate) * up).astype(BF16)

    def down():
        f = jnp.dot(act_ref[...], wd_ref[...], preferred_element_type=F32)
        vals["out"] = vals["x"] + _rms(f, gpost_ref[...], NORM_EPS)

    return [norm] + [functools.partial(chunk, c0, c1) for c0, c1 in ff_chunks] + [down], vals


LANES = 128


def _ff_chunks(d_ff, n_chunks=2):
    assert d_ff % LANES == 0
    per = -(-(d_ff // LANES) // n_chunks) * LANES
    return tuple((c0, min(c0 + per, d_ff)) for c0 in range(0, d_ff, per))


def _mixer_ffn_body(x_ref, prev_ref, wpool_ref, pscale_ref, gmix_pre_ref, gmix_post_ref,
                    gpre_ref, gpost_ref, wg_ref, wu_ref, wd_ref, out_ref, state_ref,
                    ext_ref, y_ref, act_ref, x1_ref, *, tt, n_tiles, n_total, pos0, ff_chunks):
    s = pl.program_id(0)
    live = s < n_total
    j = jnp.minimum(s, n_total - 1) % n_tiles
    pos = pos0 + j * tt + lax.broadcasted_iota(jnp.int32, (tt, 1), 0)

    @pl.when(s == 0)
    def _():
        x1_ref[...] = jnp.zeros(x1_ref.shape, F32)

    _mixer_load_prev(prev_ref, ext_ref, j)
    ffn_stages, ffn = _ffn_stages(lambda: x1_ref[(s + 1) % 2], gpre_ref, gpost_ref, wg_ref, wu_ref,
                                  wd_ref, act_ref, ff_chunks)
    mixer_stages, mixer = _mixer_stages(x_ref, state_ref, ext_ref, y_ref, wpool_ref, pscale_ref,
                                        gmix_pre_ref, gmix_post_ref, j=j, pos=pos, tt=tt,
                                        n_tiles=n_tiles, live=live)
    assert len(ffn_stages) == len(mixer_stages), (len(ffn_stages), len(mixer_stages))
    for ffn_stage, mixer_stage in zip(ffn_stages, mixer_stages):
        ffn_stage()
        mixer_stage()
    out_ref[...] = ffn["out"]
    x1_ref[s % 2] = mixer["x1"]
    mixer["store_state"]()


def _mixer_ffn(x, prev, wpool, pscale, gmix_pre, gmix_post, gpre, gpost, wg, wu, wd, *,
               layer, pos0, tt):
    b, t, d = x.shape
    n_tiles = t // tt
    n_total = b * n_tiles
    d_ff = wg.shape[2]
    body = functools.partial(_mixer_ffn_body, tt=tt, n_tiles=n_tiles, n_total=n_total, pos0=pos0,
                             ff_chunks=_ff_chunks(d_ff, n_chunks=len(POOL_WINDOWS)))

    def mixer_tile(s):
        cur = jnp.minimum(s, n_total - 1)
        return cur // n_tiles, cur % n_tiles, 0

    def ffn_tile(s):
        done = jnp.maximum(s - 1, 0)
        return done // n_tiles, done % n_tiles, 0

    state = pl.BlockSpec((None, POOL_STATE_ROWS, d), lambda s: (mixer_tile(s)[0], 0, 0))
    vec = _const_spec((1, d))
    return pl.pallas_call(
        body,
        grid=(n_total + 1,),
        in_specs=[pl.BlockSpec((None, tt, d), mixer_tile), state, _const_spec(wpool.shape),
                  vec, vec, vec, vec, vec,
                  _layer_spec(wg, layer), _layer_spec(wu, layer), _layer_spec(wd, layer)],
        out_specs=[pl.BlockSpec((None, tt, d), ffn_tile), state],
        out_shape=[jax.ShapeDtypeStruct((b, t, d), F32),
                   jax.ShapeDtypeStruct((b, POOL_STATE_ROWS, d), F32)],
        scratch_shapes=[
            pltpu.VMEM((HALO_ROWS + tt, d), F32),
            pltpu.VMEM((tt, d), F32),
            pltpu.VMEM((tt, d_ff), BF16),
            pltpu.VMEM((2, tt, d), F32),
        ],
        compiler_params=pltpu.CompilerParams(
            dimension_semantics=("arbitrary",),
            vmem_limit_bytes=VMEM_LIMIT_BYTES),
        name="pool_mixer_ffn",
    )(x, prev, wpool, pscale, gmix_pre, gmix_post, gpre, gpost, wg, wu, wd)


def _layer_spec(stack, layer):
    return pl.BlockSpec((None,) + stack.shape[1:], lambda *_: (layer, 0, 0),
                        pipeline_mode=pl.Buffered(1))


def _ffn(x, gpre, gpost, wg, wu, wd, *, layer, tm, attn=None):
    n, d = x.shape
    d_ff = wg.shape[2]
    ff_chunks = _ff_chunks(d_ff)
    row_spec = pl.BlockSpec((tm, d), lambda i: (i, 0))
    args = [x]
    in_specs = [row_spec]
    if attn is not None:
        o, wo, gmix = attn
        args += [o, wo, gmix]
        in_specs += [pl.BlockSpec((tm, o.shape[1]), lambda i: (i, 0)),
                     _const_spec(wo.shape), _const_spec((1, d))]
    args += [gpre, gpost, wg, wu, wd]
    in_specs += [_const_spec((1, d)), _const_spec((1, d)),
                 _layer_spec(wg, layer), _layer_spec(wu, layer), _layer_spec(wd, layer)]
    body = functools.partial(_ffn_body, with_attn=attn is not None, ff_chunks=ff_chunks)
    return pl.pallas_call(
        body,
        grid=(n // tm,),
        in_specs=in_specs,
        out_specs=row_spec,
        out_shape=jax.ShapeDtypeStruct((n, d), F32),
        scratch_shapes=[pltpu.VMEM((tm, d_ff), BF16)],
        compiler_params=pltpu.CompilerParams(
            dimension_semantics=("arbitrary",),
            vmem_limit_bytes=VMEM_LIMIT_BYTES),
        name="attn_out_ffn" if attn is not None else "ffn",
    )(*args)


def _proj_body(x_ref, gkv_ref, gq_ref, wk_ref, wv_ref, wq_ref,
               k_ref, v_ref, q_ref, kb_ref, vtb_ref, qb_ref):
    x = x_ref[...]
    src = _rms(x, gkv_ref[...], NORM_EPS).astype(BF16)
    k = jnp.dot(src, wk_ref[...], preferred_element_type=F32)
    k_ref[...] = k
    kb_ref[...] = k.astype(BF16)
    v = jnp.dot(src, wv_ref[...], preferred_element_type=F32)
    v_ref[...] = v
    vtb_ref[...] = v.T.astype(BF16)
    h = _rms(x, gq_ref[...], NORM_EPS).astype(BF16)
    q = jnp.dot(h, wq_ref[...], preferred_element_type=F32) * (HEAD_DIM ** -0.5)
    q_ref[...] = q
    qb_ref[...] = (q * LOG2E).astype(BF16)


def _proj(x, gkv, gq, wk, wv, wq, *, tm):
    n, d = x.shape
    w = wk.shape[1]
    row_in = pl.BlockSpec((tm, d), lambda i: (i, 0))
    row_out = pl.BlockSpec((tm, w), lambda i: (i, 0))
    col_out = pl.BlockSpec((w, tm), lambda i: (0, i))
    row_f32, row_bf16 = jax.ShapeDtypeStruct((n, w), F32), jax.ShapeDtypeStruct((n, w), BF16)
    return pl.pallas_call(
        _proj_body,
        grid=(n // tm,),
        in_specs=[row_in, _const_spec((1, d)), _const_spec((1, d)),
                  _const_spec(wk.shape), _const_spec(wv.shape), _const_spec(wq.shape)],
        out_specs=[row_out, row_out, row_out, row_out, col_out, row_out],
        out_shape=[row_f32, row_f32, row_f32, row_bf16,
                   jax.ShapeDtypeStruct((w, n), BF16), row_bf16],
        compiler_params=pltpu.CompilerParams(
            dimension_semantics=("arbitrary",),
            vmem_limit_bytes=VMEM_LIMIT_BYTES),
        name="kvq_proj",
    )(x, gkv, gq, wk, wv, wq)


def _lambda(lq_ref, lam_init):
    lq = lq_ref[...]
    a = jnp.sum(lq[0:1] * lq[1:2], axis=-1, keepdims=True)
    b = jnp.sum(lq[2:3] * lq[3:4], axis=-1, keepdims=True)
    return jnp.exp(a) - jnp.exp(b) + lam_init


def _diff_finalize(o1, o2, lq_ref, gsub_ref, lam_init):
    od = o1 - _lambda(lq_ref, lam_init) * o2
    return _rms(od, gsub_ref[...], SUBLN_EPS) * (1.0 - lam_init)


def _online_update(m_ref, l_ref, acc_ref, m_cur):
    m_prev = m_ref[...]
    m_new = jnp.maximum(m_prev, m_cur)
    alpha = jnp.exp(m_prev - m_new)
    m_ref[...] = m_new
    l_ref[...] = alpha * l_ref[...]
    acc_ref[...] = alpha * acc_ref[...]
    return m_new


def _alibi_feature_rows(slopes):
    cs = slopes * LOG2E
    c1 = cs.astype(BF16).astype(F32)
    c2 = (cs - c1).astype(BF16).astype(F32)
    c3 = (cs - c1 - c2).astype(BF16).astype(F32)
    c = jnp.stack([c1, c2, c3], axis=1)
    half = N_POS_FEATURES // 2
    pad = jnp.zeros((slopes.shape[0], V_DIM - N_POS_FEATURES), F32)
    zeros = jnp.zeros((slopes.shape[0], half), F32)
    q_rows = jnp.concatenate([POS_RADIX * c, c, zeros, pad], axis=1)
    k_rows = jnp.concatenate([zeros, -POS_RADIX * c, -c, pad], axis=1)
    return q_rows, k_rows


def _prompt_phases(item, q_ref, k_ref, vt_ref, qrow_ref, krow_ref, lq_ref, gsub_ref, o_ref,
                   pos_ref, mask_ref, kaug_ref, qfeat_ref, s_ref, p_ref, *, blk, lam_init):
    seq = q_ref.shape[0]
    nq = seq // blk
    slot = item % nq
    head = (item // nq) % N_HEADS
    half = N_POS_FEATURES // 2
    contract_last = (((1,), (1,)), ((), ()))

    def init_tables():
        pos = lax.broadcasted_iota(jnp.int32, (seq, V_DIM), 0)
        lane = lax.broadcasted_iota(jnp.int32, (seq, V_DIM), 1)
        hi = (pos >> _log2(POS_RADIX)).astype(F32)
        lo = (pos & (POS_RADIX - 1)).astype(F32)
        third = half // 2
        is_hi = (lane < third) | ((lane >= half) & (lane < half + third))
        pos_ref[...] = jnp.where(lane >= N_POS_FEATURES, 0.0, jnp.where(is_hi, hi, lo))
        key = lax.broadcasted_iota(jnp.int32, (blk, 2 * blk), 0)
        qry = lax.broadcasted_iota(jnp.int32, (blk, 2 * blk), 1) & (blk - 1)
        mask_ref[...] = jnp.where(key <= qry, 0.0, NEG_INF)

    def prepare_head():
        lane = lax.broadcasted_iota(jnp.int32, (seq, V_DIM), 1)
        pos_feat = pos_ref[...]
        kaug_ref[:, 0:V_DIM] = k_ref[...]
        kaug_ref[:, V_DIM:] = jnp.where(lane < half, pos_feat,
                                        krow_ref[pl.ds(head, 1), :]).astype(BF16)
        qfeat_ref[...] = jnp.where(lane < half, qrow_ref[pl.ds(head, 1), :], pos_feat).astype(BF16)

    def query_block(i):
        lane = lax.broadcasted_iota(jnp.int32, (blk, V_DIM), 1)
        n_keys = (i + 1) * blk
        q = q_ref[i * blk:(i + 1) * blk, :]
        zero = jnp.zeros_like(q)
        qf = qfeat_ref[i * blk:(i + 1) * blk, :]
        q2 = jnp.concatenate(
            [jnp.concatenate([jnp.where(lane < HEAD_DIM, q, zero), qf], axis=1),
             jnp.concatenate([jnp.where(lane >= HEAD_DIM, q, zero), qf], axis=1)], axis=0)
        s_ref[0:n_keys, :] = lax.dot_general(kaug_ref[0:n_keys, :], q2, contract_last,
                                             preferred_element_type=F32)
        diag = slice(i * blk, (i + 1) * blk)
        s_ref[diag, :] = s_ref[diag, :] + mask_ref[...]
        m = None
        for j in range(i + 1):
            m_j = jnp.max(s_ref[j * blk:(j + 1) * blk, :], axis=0, keepdims=True)
            m = m_j if m is None else jnp.maximum(m, m_j)
        l = jnp.zeros_like(m)
        for j in range(i + 1):
            rows = slice(j * blk, (j + 1) * blk)
            p = jnp.exp2(s_ref[rows, :] - m)
            l = l + jnp.sum(p, axis=0, keepdims=True)
            p_ref[rows, :] = p.astype(BF16)
        ot = jnp.dot(vt_ref[:, 0:n_keys], p_ref[0:n_keys, :], preferred_element_type=F32) / l
        o = _diff_finalize(ot[:, :blk].T, ot[:, blk:].T, lq_ref, gsub_ref, lam_init)
        o_ref[i * blk:(i + 1) * blk, :] = o.astype(o_ref.dtype)

    def prepare():
        pl.when(item == 0)(init_tables)
        pl.when(slot == 0)(prepare_head)

    return prepare, slot, [functools.partial(query_block, i) for i in range(nq)]


def _prompt_scratch(seq, blk):
    return [
        pltpu.VMEM((seq, V_DIM), F32),
        pltpu.VMEM((blk, 2 * blk), F32),
        pltpu.VMEM((seq, 2 * V_DIM), BF16),
        pltpu.VMEM((seq, V_DIM), BF16),
        pltpu.VMEM((seq, 2 * blk), F32),
        pltpu.VMEM((seq, 2 * blk), BF16),
    ]


def _sample_phases(g, q_ref, qprev_ref, slope_ref, k_refs, v_refs, knew_ref, vnew_ref, lq_ref,
                   gsub_ref, o_ref, bias_ref, p_ref, mp_ref, lp_ref, m_ref, l_ref, acc_ref, *,
                   n_steps, n_groups, past_len, dec_seq, lam_init):
    pages_per_step = len(k_refs)
    prev_step = _value_group(g) % n_steps
    n_rows = q_ref.shape[1]
    slope = slope_ref[:, 0:1]
    contract_last = (((1,), (1,)), ((), ()))

    def row_col_ids(shape):
        r = lax.broadcasted_iota(jnp.int32, shape, 0)
        c = lax.broadcasted_iota(jnp.int32, shape, 1)
        head_q = (r >> _log2(dec_seq)) & (N_HEADS - 1)
        return head_q, r & (dec_seq - 1), c >> _log2(N_HEADS), c & (N_HEADS - 1)

    def init_tables():
        hq, tok, key, hk = row_col_ids((n_rows, PAGE_ROWS))
        bias_ref[...] = jnp.where(hq == hk, -slope * (tok - key).astype(F32), NEG_INF)
        p_ref[...] = jnp.zeros(p_ref.shape, p_ref.dtype)
        mp_ref[...] = jnp.full(mp_ref.shape, NEG_INF, F32)
        lp_ref[...] = jnp.zeros(lp_ref.shape, F32)

    def reset_row():
        m_ref[...] = jnp.full(m_ref.shape, NEG_INF, F32)
        l_ref[...] = jnp.zeros(l_ref.shape, F32)
        acc_ref[...] = jnp.zeros(acc_ref.shape, F32)

    def prepare():
        pl.when(g == 0)(init_tables)
        pl.when(prev_step == 0)(reset_row)

    def body():
        m_cur = mp_ref[0]
        for i in range(1, pages_per_step):
            m_cur = jnp.maximum(m_cur, mp_ref[i])
        m_new = _online_update(m_ref, l_ref, acc_ref, m_cur)
        l_add = jnp.zeros(l_ref.shape, F32)
        acc_add = jnp.zeros(acc_ref.shape, F32)
        for i in range(pages_per_step):
            w = jnp.exp(mp_ref[i] - m_new)
            l_add = l_add + w * lp_ref[i]
            acc_add = acc_add + w * jnp.dot(p_ref[i], v_refs[i][...], preferred_element_type=F32)
        l_ref[...] += l_add
        acc_ref[...] += acc_add

        q = q_ref[0]
        bias = bias_ref[...]
        step = _score_group(g, n_groups) % n_steps
        for i in range(pages_per_step):
            s = lax.dot_general(q, k_refs[i][...], contract_last, preferred_element_type=F32)
            t = s + bias
            m_page = jnp.max(t, axis=-1, keepdims=True)
            p = jnp.exp(t - m_page)
            p_ref[i] = p.astype(p_ref.dtype)
            lp_ref[i] = jnp.sum(p, axis=-1, keepdims=True)
            page_pos = (step * pages_per_step + i) * PAGE_SIZE
            mp_ref[i] = m_page - slope * (past_len - page_pos).astype(F32)

    def finish_row():
        qp = qprev_ref[0]
        hq, tok, key, hk = row_col_ids((n_rows, knew_ref.shape[1]))
        valid = (hq == hk) & (key <= tok)
        bias_new = jnp.where(valid, -slope * (tok - key).astype(F32), NEG_INF)
        s = lax.dot_general(qp, knew_ref[0], contract_last, preferred_element_type=F32)
        t = s + bias_new
        m_fin = _online_update(m_ref, l_ref, acc_ref, jnp.max(t, axis=-1, keepdims=True))
        p = jnp.exp(t - m_fin)
        l = l_ref[...] + jnp.sum(p, axis=-1, keepdims=True)
        acc = acc_ref[...] + jnp.dot(p, vnew_ref[0], preferred_element_type=F32)
        o = acc / l
        half = n_rows // 2
        o_ref[0] = _diff_finalize(o[:half], o[half:], lq_ref, gsub_ref, lam_init)

    def finish():
        pl.when((g > 0) & (prev_step == n_steps - 1))(finish_row)

    return prepare, body, finish


def _score_group(g, n_groups):
    return jnp.minimum(g, n_groups - 1)


def _value_group(g):
    return jnp.maximum(g - 1, 0)


def _attention_body(pt_ref, q_ref, qprev_ref, slope_ref, ck_ref, cv_ref, knew_ref, vnew_ref, lq_ref,
                    gsub_ref, qb_ref, kb_ref, vtb_ref, qrow_ref, krow_ref, os_ref, op_ref,
                    kbuf_ref, vbuf_ref, ksem, vsem, *scratch, pages_per_step, n_groups, n_items,
                    blk, sample_args, lam_init):
    sample_scratch = scratch[:N_SAMPLE_SCRATCH]
    prompt_scratch = scratch[N_SAMPLE_SCRATCH:]
    g = pl.program_id(0)

    def page_copies(cache_ref, buf_ref, sem, group):
        slot = group % PAGE_RING
        copies = []
        for i in range(pages_per_step):
            row0 = pl.multiple_of(pt_ref[group * pages_per_step + i] * PAGE_ROWS, PAGE_ROWS)
            copies.append(pltpu.make_async_copy(
                cache_ref.at[pl.ds(row0, PAGE_ROWS), :], buf_ref.at[slot, i], sem.at[slot]))
        return copies

    def start_keys(group):
        for i, copy in enumerate(page_copies(ck_ref, kbuf_ref, ksem, group)):
            copy.start(priority=i % 2)

    def start_values(group):
        for i, copy in enumerate(page_copies(cv_ref, vbuf_ref, vsem, group)):
            copy.start(priority=i % 2)

    @pl.when(g == 0)
    def _():
        start_keys(0)
        start_values(0)
        if n_groups > 1:
            start_keys(1)
        vbuf_ref[PAGE_RING - 1] = jnp.zeros(vbuf_ref.shape[1:], vbuf_ref.dtype)

    pl.when(g + 2 < n_groups)(lambda: start_keys(g + 2))
    pl.when(g + 1 < n_groups)(lambda: start_values(g + 1))

    @pl.when(g < n_groups)
    def _():
        for copy in page_copies(ck_ref, kbuf_ref, ksem, g):
            copy.wait()

    @pl.when(g > 0)
    def _():
        for copy in page_copies(cv_ref, vbuf_ref, vsem, g - 1):
            copy.wait()

    key_slot = g % PAGE_RING
    value_slot = (g + PAGE_RING - 1) % PAGE_RING
    k_refs = [kbuf_ref.at[key_slot, i] for i in range(pages_per_step)]
    v_refs = [vbuf_ref.at[value_slot, i] for i in range(pages_per_step)]
    sample_prepare, sample_body, sample_finish = _sample_phases(
        g, q_ref, qprev_ref, slope_ref, k_refs, v_refs, knew_ref, vnew_ref, lq_ref, gsub_ref,
        os_ref, *sample_scratch, lam_init=lam_init, **sample_args)
    has_item = g < n_items
    prompt_prepare, slot, slot_bodies = _prompt_phases(
        jnp.minimum(g, n_items - 1), qb_ref, kb_ref, vtb_ref, qrow_ref, krow_ref, lq_ref,
        gsub_ref, op_ref, *prompt_scratch, blk=blk, lam_init=lam_init)

    sample_prepare()
    pl.when(has_item)(prompt_prepare)

    def both(slot_body):
        sample_body()
        slot_body()

    for j, slot_body in enumerate(slot_bodies):
        pl.when(has_item & (slot == j))(functools.partial(both, slot_body))
    pl.when(jnp.logical_not(has_item))(sample_body)
    sample_finish()


def _attention(qb, kb, vtb, q_s, k_s, v_s, cache_k, cache_v, page_table, slopes, lq, gsub, *,
               batch, seq, lam_init):
    dec_batch, dec_seq, _ = q_s.shape
    n_pages = page_table.shape[1]
    pps = PAGES_PER_STEP
    n_steps = n_pages // pps
    n_groups = dec_batch * n_steps
    n_rows = 2 * N_HEADS * dec_seq
    blk = ATTN_BLOCK
    n_slots = seq // blk
    n_items = batch * N_HEADS * n_slots
    assert n_items <= n_groups + 1, (n_items, n_groups)

    q5 = q_s.reshape(dec_batch, dec_seq, N_HEADS, 2, HEAD_DIM)
    zeros = jnp.zeros_like(q5[..., 0, :])
    qm = jnp.stack([jnp.concatenate([q5[..., 0, :], zeros], axis=-1),
                    jnp.concatenate([zeros, q5[..., 1, :]], axis=-1)], axis=1)
    qall = qm.transpose(0, 1, 3, 2, 4).reshape(dec_batch, n_rows, V_DIM)
    row_slope = jnp.broadcast_to(jnp.tile(jnp.repeat(slopes, dec_seq), 2)[:, None], (n_rows, 128))

    def pad_new(a):
        a = a.reshape(dec_batch, dec_seq, N_HEADS, V_DIM)
        a = jnp.pad(a, ((0, 0), (0, NEW_KEY_PAD - dec_seq), (0, 0), (0, 0)))
        return a.reshape(dec_batch, NEW_KEY_PAD * N_HEADS, V_DIM)

    knew, vnew = pad_new(k_s), pad_new(v_s)
    ck = cache_k.reshape(-1, V_DIM)
    cv = cache_v.reshape(-1, V_DIM)
    q_rows, k_rows = _alibi_feature_rows(slopes)

    score_row = lambda g, pt: (_score_group(g, n_groups) // n_steps, 0, 0)
    value_row = lambda g, pt: (_value_group(g) // n_steps, 0, 0)
    const2 = lambda g, pt: (0, 0)

    def pair(g):
        p = jnp.minimum(g, n_items - 1) // n_slots
        return p // N_HEADS, p % N_HEADS

    head_rows = pl.BlockSpec((seq, V_DIM), lambda g, pt: pair(g))
    head_cols = pl.BlockSpec((V_DIM, seq), lambda g, pt: pair(g)[::-1])
    cache_spec = pl.BlockSpec(memory_space=pl.ANY)
    page_ring = pltpu.VMEM((PAGE_RING, pps, PAGE_ROWS, V_DIM), cache_k.dtype)
    sample_scratch = [
        pltpu.VMEM((n_rows, PAGE_ROWS), F32),
        pltpu.VMEM((pps, n_rows, PAGE_ROWS), BF16),
        pltpu.VMEM((pps, n_rows, 1), F32),
        pltpu.VMEM((pps, n_rows, 1), F32),
        pltpu.VMEM((n_rows, 1), F32),
        pltpu.VMEM((n_rows, 1), F32),
        pltpu.VMEM((n_rows, V_DIM), F32),
    ]
    assert len(sample_scratch) == N_SAMPLE_SCRATCH
    body = functools.partial(
        _attention_body, pages_per_step=pps, n_groups=n_groups, n_items=n_items, blk=blk,
        lam_init=lam_init,
        sample_args=dict(n_steps=n_steps, n_groups=n_groups, past_len=n_pages * PAGE_SIZE,
                         dec_seq=dec_seq))
    grid_spec = pltpu.PrefetchScalarGridSpec(
        num_scalar_prefetch=1,
        grid=(n_groups + 1,),
        in_specs=[
            pl.BlockSpec((1, n_rows, V_DIM), score_row),
            pl.BlockSpec((1, n_rows, V_DIM), value_row),
            pl.BlockSpec(row_slope.shape, const2),
            cache_spec, cache_spec,
            pl.BlockSpec((1,) + knew.shape[1:], value_row),
            pl.BlockSpec((1,) + vnew.shape[1:], value_row),
            pl.BlockSpec(lq.shape, const2),
            pl.BlockSpec(gsub.shape, const2),
            head_rows, head_rows, head_cols,
            pl.BlockSpec(q_rows.shape, const2),
            pl.BlockSpec(k_rows.shape, const2),
        ],
        out_specs=[pl.BlockSpec((1, n_rows // 2, V_DIM), value_row), head_rows],
        scratch_shapes=[
            page_ring, page_ring,
            pltpu.SemaphoreType.DMA((PAGE_RING,)), pltpu.SemaphoreType.DMA((PAGE_RING,)),
            *sample_scratch, *_prompt_scratch(seq, blk),
        ],
    )
    o_s, o_p = pl.pallas_call(
        body,
        grid_spec=grid_spec,
        out_shape=[jax.ShapeDtypeStruct((dec_batch, n_rows // 2, V_DIM), F32),
                   jax.ShapeDtypeStruct(qb.shape, BF16)],
        compiler_params=pltpu.CompilerParams(
            dimension_semantics=("arbitrary",),
            vmem_limit_bytes=VMEM_LIMIT_BYTES),
        name="diff_attention",
    )(page_table.reshape(-1), qall, qall, row_slope, ck, cv,
      knew, vnew, lq, gsub, qb, kb, vtb, q_rows, k_rows)
    o_s = o_s.reshape(dec_batch, N_HEADS, dec_seq, V_DIM).transpose(0, 2, 1, 3)
    return o_p, o_s.reshape(dec_batch * dec_seq, N_HEADS * V_DIM)


def kernel(x_prompt, x_sample, state_pool, cache_k, cache_v, page_table, w_pool, pool_scale, g_kv,
           w_k, w_v, w_q, w_o, lambda_qk, g_sub, g_mix_pre, g_mix_post, g_ffn_pre, g_ffn_post,
           w_gate, w_up, w_down):
    batch, seq, d = x_prompt.shape
    dec_batch, dec_seq, _ = x_sample.shape
    past_len = page_table.shape[1] * PAGE_SIZE
    depth = g_mix_pre.shape[0]
    assert depth == 2 and w_pool.shape[0] == N_A_LAYERS and d == D_MODEL

    row = lambda a: a.reshape(1, -1)
    wpool = w_pool[0].astype(BF16)
    wg, wu, wd = (w.astype(BF16) for w in (w_gate, w_up, w_down))
    wk, wv, wq, wo = (w.astype(BF16) for w in (w_k, w_v, w_q[0], w_o[0]))
    slopes = jnp.exp2(-8.0 * (jnp.arange(N_HEADS, dtype=F32) + 1.0) / N_HEADS)
    lam_init = 0.8 - 0.6 * math.exp(-0.3 * 1)
    lq = lambda_qk[0].astype(F32)
    gsub = row(g_sub[0])

    def layer0(x, prev, pos0, tt, tm, rows=1):
        b, t, _ = x.shape
        x1, state = _mixer(x, prev, wpool, row(pool_scale[0]), row(g_mix_pre[0]),
                           row(g_mix_post[0]), pos0=pos0, tt=tt, rows=rows)
        x2 = _ffn(x1.reshape(b * t, d), row(g_ffn_pre[0]), row(g_ffn_post[0]),
                  wg, wu, wd, layer=0, tm=tm)
        return x2, state

    def layer1_tail(x2, o, tm):
        return _ffn(x2, row(g_ffn_pre[1]), row(g_ffn_post[1]), wg, wu, wd, layer=1, tm=tm,
                    attn=(o, wo, row(g_mix_post[1])))

    zero_prev = jnp.zeros((batch, POOL_STATE_ROWS, d), F32)
    x2p, state_p = _mixer_ffn(x_prompt, zero_prev, wpool, row(pool_scale[0]), row(g_mix_pre[0]),
                              row(g_mix_post[0]), row(g_ffn_pre[0]), row(g_ffn_post[0]),
                              wg, wu, wd, layer=0, pos0=0, tt=TOKEN_TILE)
    x2p = x2p.reshape(batch * seq, d)
    kp, vp, _, kpb, vtpb, qpb = _proj(x2p, row(g_kv), row(g_mix_pre[1]), wk, wv, wq, tm=TOKEN_TILE)
    n_s = dec_batch * dec_seq
    x2s, state_s = layer0(x_sample, state_pool[0], past_len, dec_seq, n_s, rows=SAMPLE_MIXER_ROWS)
    ks, vs, qs, _, _, _ = _proj(x2s, row(g_kv), row(g_mix_pre[1]), wk, wv, wq, tm=n_s)

    op, os_ = _attention(qpb, kpb, vtpb, qs.reshape(dec_batch, dec_seq, -1),
                         ks.reshape(dec_batch, dec_seq, -1), vs.reshape(dec_batch, dec_seq, -1),
                         cache_k, cache_v, page_table, slopes, lq, gsub,
                         batch=batch, seq=seq, lam_init=lam_init)
    y_prompt = layer1_tail(x2p, op, TOKEN_TILE).reshape(batch, seq, d)
    y_sample = layer1_tail(x2s, os_, n_s).reshape(dec_batch, dec_seq, d)

    kv_shape_p = (batch, seq, N_HEADS, V_DIM)
    kv_shape_s = (dec_batch, dec_seq, N_HEADS, V_DIM)
    return (y_prompt, y_sample, state_p[None], state_s[None],
            kp.reshape(kv_shape_p), vp.reshape(kv_shape_p),
            ks.reshape(kv_shape_s), vs.reshape(kv_shape_s))
```

```python
import functools
import math

import jax
import jax.numpy as jnp
from jax import lax
from jax.experimental import pallas as pl
from jax.experimental.pallas import tpu as pltpu

F32 = jnp.float32
BF16 = jnp.bfloat16

D_MODEL = 1024
POOL_WINDOWS = (2, 4, 8, 16)
POOL_GROUP_DIM = D_MODEL // len(POOL_WINDOWS)
POOL_STATE_ROWS = max(POOL_WINDOWS) - 1
HALO_ROWS = POOL_STATE_ROWS + 1
HEAD_DIM = 64
V_DIM = 2 * HEAD_DIM
N_HEADS = D_MODEL // V_DIM
PAGE_SIZE = 128
PAGE_ROWS = PAGE_SIZE * N_HEADS
NORM_EPS = 1e-6
SUBLN_EPS = 1e-5
NEG_INF = -1e30
N_A_LAYERS = 1

VMEM_LIMIT_BYTES = 56 * 1024 * 1024
TOKEN_TILE = 512
ATTN_BLOCK = 256
PAGES_PER_STEP = 8
PAGE_RING = 3
N_SAMPLE_SCRATCH = 7
SAMPLE_MIXER_ROWS = 8
NEW_KEY_PAD = 16
LOG2E = math.log2(math.e)
POS_RADIX = 128
N_POS_FEATURES = 12


def _rms(xf, g, eps):
    return xf * lax.rsqrt(jnp.mean(xf * xf, axis=-1, keepdims=True) + eps) * g


def _const_spec(shape):
    zeros = (0,) * len(shape)
    return pl.BlockSpec(shape, lambda *_: zeros, pipeline_mode=pl.Buffered(1))


def _log2(n):
    assert n & (n - 1) == 0, n
    return n.bit_length() - 1


def _mixer_load_prev(prev_ref, ext, j):
    @pl.when(j == 0)
    def _():
        ext[0:1, :] = jnp.zeros((1, D_MODEL), F32)
        ext[1:HALO_ROWS, :] = prev_ref[...]


def _mixer_tile(x_ref, prev_ref, state_ref, ext, y, wpool_ref, pscale_ref, gpre_ref, gpost_ref, *,
                j, pos, tt, n_tiles, live=True, load_prev=True):
    if load_prev:
        _mixer_load_prev(prev_ref, ext, j)
    stages, result = _mixer_stages(x_ref, state_ref, ext, y, wpool_ref, pscale_ref, gpre_ref,
                                   gpost_ref, j=j, pos=pos, tt=tt, n_tiles=n_tiles, live=live)
    for stage in stages:
        stage()
    return result["x1"], result["store_state"]


def _mixer_stages(x_ref, state_ref, ext, y, wpool_ref, pscale_ref, gpre_ref, gpost_ref, *,
                  j, pos, tt, n_tiles, live):
    vals = {}

    def norm():
        vals["x"] = x_ref[...]
        vals["h"] = _rms(vals["x"], gpre_ref[...], NORM_EPS)
        ext[HALO_ROWS:HALO_ROWS + tt, :] = vals["h"]

    def group(g, w):
        cols = slice(g * POOL_GROUP_DIM, (g + 1) * POOL_GROUP_DIM)
        e = vals["h"][:, cols]
        s = e
        for k in range(1, w):
            s = s + ext[HALO_ROWS - k:HALO_ROWS - k + tt, cols]
        inv_cnt = 1.0 / jnp.minimum(w, pos + 1).astype(F32)
        diff = (s * inv_cnt - e).astype(BF16)
        yg = jnp.dot(diff, wpool_ref[g], preferred_element_type=F32)
        y[:, cols] = yg * pscale_ref[:, cols]

    def output():
        vals["x1"] = vals["x"] + _rms(y[...], gpost_ref[...], NORM_EPS)
        assert n_tiles == 1 or tt >= HALO_ROWS
        state_rows = ext[tt + 1:tt + HALO_ROWS, :]
        if n_tiles > 1:
            ext[0:HALO_ROWS, :] = ext[tt:tt + HALO_ROWS, :]

        def store_state():
            @pl.when(live & (j == n_tiles - 1))
            def _():
                state_ref[...] = state_rows

        vals["store_state"] = store_state

    stages = [norm] + [functools.partial(group, g, w) for g, w in enumerate(POOL_WINDOWS)]
    return stages + [output], vals


def _mixer_body(x_ref, prev_ref, wpool_ref, pscale_ref, gpre_ref, gpost_ref,
                out_ref, state_ref, ext_ref, y_ref, *, rows, tt, n_tiles, pos0):
    j = pl.program_id(1)
    pos = pos0 + j * tt + lax.broadcasted_iota(jnp.int32, (tt, 1), 0)
    for r in range(rows):
        x1, store_state = _mixer_tile(
            x_ref.at[r], prev_ref.at[r], state_ref.at[r], ext_ref.at[r], y_ref.at[r], wpool_ref,
            pscale_ref, gpre_ref, gpost_ref, j=j, pos=pos, tt=tt, n_tiles=n_tiles)
        out_ref[r] = x1
        store_state()


def _mixer(x, prev, wpool, pscale, gpre, gpost, *, pos0, tt, rows=1):
    b, t, d = x.shape
    n_tiles = t // tt
    body = functools.partial(_mixer_body, rows=rows, tt=tt, n_tiles=n_tiles, pos0=pos0)
    return pl.pallas_call(
        body,
        grid=(b // rows, n_tiles),
        in_specs=[
            pl.BlockSpec((rows, tt, d), lambda i, j: (i, j, 0)),
            pl.BlockSpec((rows, POOL_STATE_ROWS, d), lambda i, j: (i, 0, 0)),
            _const_spec(wpool.shape),
            _const_spec((1, d)),
            _const_spec((1, d)),
            _const_spec((1, d)),
        ],
        out_specs=[
            pl.BlockSpec((rows, tt, d), lambda i, j: (i, j, 0)),
            pl.BlockSpec((rows, POOL_STATE_ROWS, d), lambda i, j: (i, 0, 0)),
        ],
        out_shape=[
            jax.ShapeDtypeStruct((b, t, d), F32),
            jax.ShapeDtypeStruct((b, POOL_STATE_ROWS, d), F32),
        ],
        scratch_shapes=[
            pltpu.VMEM((rows, HALO_ROWS + tt, d), F32),
            pltpu.VMEM((rows, tt, d), F32),
        ],
        compiler_params=pltpu.CompilerParams(
            dimension_semantics=("arbitrary", "arbitrary"),
            vmem_limit_bytes=VMEM_LIMIT_BYTES),
        name="pool_mixer",
    )(x, prev, wpool, pscale, gpre, gpost)


def _ffn_body(*refs, with_attn, ff_chunks):
    if with_attn:
        (x_ref, o_ref, wo_ref, gmix_ref, gpre_ref, gpost_ref, wg_ref, wu_ref, wd_ref,
         out_ref, act_ref) = refs
    else:
        x_ref, gpre_ref, gpost_ref, wg_ref, wu_ref, wd_ref, out_ref, act_ref = refs
    x = x_ref[...]
    if with_attn:
        y = jnp.dot(o_ref[...].astype(BF16), wo_ref[...], preferred_element_type=F32)
        x = x + _rms(y, gmix_ref[...], NORM_EPS)
    out_ref[...] = _ffn_tile(x, gpre_ref, gpost_ref, wg_ref, wu_ref, wd_ref, act_ref, ff_chunks)


def _ffn_tile(x, gpre_ref, gpost_ref, wg_ref, wu_ref, wd_ref, act_ref, ff_chunks):
    stages, result = _ffn_stages(lambda: x, gpre_ref, gpost_ref, wg_ref, wu_ref, wd_ref, act_ref,
                                 ff_chunks)
    for stage in stages:
        stage()
    return result["out"]


def _ffn_stages(load_x, gpre_ref, gpost_ref, wg_ref, wu_ref, wd_ref, act_ref, ff_chunks):
    vals = {}

    def norm():
        vals["x"] = load_x()
        vals["h"] = _rms(vals["x"], gpre_ref[...], NORM_EPS).astype(BF16)

    def chunk(c0, c1):
        gate = jnp.dot(vals["h"], wg_ref[:, c0:c1], preferred_element_type=F32)
        up = jnp.dot(vals["h"], wu_ref[:, c0:c1], preferred_element_type=F32)
        act_ref[:, c0:c1] = (gate * jax.nn.sigmoid(gate) * up).astype(BF16)

    def down():
        f = jnp.dot(act_ref[...], wd_ref[...], preferred_element_type=F32)
        vals["out"] = vals["x"] + _rms(f, gpost_ref[...], NORM_EPS)

    return [norm] + [functools.partial(chunk, c0, c1) for c0, c1 in ff_chunks] + [down], vals


LANES = 128


def _ff_chunks(d_ff, n_chunks=2):
    assert d_ff % LANES == 0
    per = -(-(d_ff // LANES) // n_chunks) * LANES
    return tuple((c0, min(c0 + per, d_ff)) for c0 in range(0, d_ff, per))


def _mixer_ffn_body(x_ref, prev_ref, wpool_ref, pscale_ref, gmix_pre_ref, gmix_post_ref,
                    gpre_ref, gpost_ref, wg_ref, wu_ref, wd_ref, out_ref, state_ref,
                    ext_ref, y_ref, act_ref, x1_ref, *, tt, n_tiles, n_total, pos0, ff_chunks):
    s = pl.program_id(0)
    live = s < n_total
    j = jnp.minimum(s, n_total - 1) % n_tiles
    pos = pos0 + j * tt + lax.broadcasted_iota(jnp.int32, (tt, 1), 0)

    @pl.when(s == 0)
    def _():
        x1_ref[...] = jnp.zeros(x1_ref.shape, F32)

    _mixer_load_prev(prev_ref, ext_ref, j)
    ffn_stages, ffn = _ffn_stages(lambda: x1_ref[(s + 1) % 2], gpre_ref, gpost_ref, wg_ref, wu_ref,
                                  wd_ref, act_ref, ff_chunks)
    mixer_stages, mixer = _mixer_stages(x_ref, state_ref, ext_ref, y_ref, wpool_ref, pscale_ref,
                                        gmix_pre_ref, gmix_post_ref, j=j, pos=pos, tt=tt,
                                        n_tiles=n_tiles, live=live)
    assert len(ffn_stages) == len(mixer_stages), (len(ffn_stages), len(mixer_stages))
    for ffn_stage, mixer_stage in zip(ffn_stages, mixer_stages):
        ffn_stage()
        mixer_stage()
    out_ref[...] = ffn["out"]
    x1_ref[s % 2] = mixer["x1"]
    mixer["store_state"]()


def _mixer_ffn(x, prev, wpool, pscale, gmix_pre, gmix_post, gpre, gpost, wg, wu, wd, *,
               layer, pos0, tt):
    b, t, d = x.shape
    n_tiles = t // tt
    n_total = b * n_tiles
    d_ff = wg.shape[2]
    body = functools.partial(_mixer_ffn_body, tt=tt, n_tiles=n_tiles, n_total=n_total, pos0=pos0,
                             ff_chunks=_ff_chunks(d_ff, n_chunks=len(POOL_WINDOWS)))

    def mixer_tile(s):
        cur = jnp.minimum(s, n_total - 1)
        return cur // n_tiles, cur % n_tiles, 0

    def ffn_tile(s):
        done = jnp.maximum(s - 1, 0)
        return done // n_tiles, done % n_tiles, 0

    state = pl.BlockSpec((None, POOL_STATE_ROWS, d), lambda s: (mixer_tile(s)[0], 0, 0))
    vec = _const_spec((1, d))
    return pl.pallas_call(
        body,
        grid=(n_total + 1,),
        in_specs=[pl.BlockSpec((None, tt, d), mixer_tile), state, _const_spec(wpool.shape),
                  vec, vec, vec, vec, vec,
                  _layer_spec(wg, layer), _layer_spec(wu, layer), _layer_spec(wd, layer)],
        out_specs=[pl.BlockSpec((None, tt, d), ffn_tile), state],
        out_shape=[jax.ShapeDtypeStruct((b, t, d), F32),
                   jax.ShapeDtypeStruct((b, POOL_STATE_ROWS, d), F32)],
        scratch_shapes=[
            pltpu.VMEM((HALO_ROWS + tt, d), F32),
            pltpu.VMEM((tt, d), F32),
            pltpu.VMEM((tt, d_ff), BF16),
            pltpu.VMEM((2, tt, d), F32),
        ],
        compiler_params=pltpu.CompilerParams(
            dimension_semantics=("arbitrary",),
            vmem_limit_bytes=VMEM_LIMIT_BYTES),
        name="pool_mixer_ffn",
    )(x, prev, wpool, pscale, gmix_pre, gmix_post, gpre, gpost, wg, wu, wd)


def _layer_spec(stack, layer):
    return pl.BlockSpec((None,) + stack.shape[1:], lambda *_: (layer, 0, 0),
                        pipeline_mode=pl.Buffered(1))


def _ffn(x, gpre, gpost, wg, wu, wd, *, layer, tm, attn=None):
    n, d = x.shape
    d_ff = wg.shape[2]
    ff_chunks = _ff_chunks(d_ff)
    row_spec = pl.BlockSpec((tm, d), lambda i: (i, 0))
    args = [x]
    in_specs = [row_spec]
    if attn is not None:
        o, wo, gmix = attn
        args += [o, wo, gmix]
        in_specs += [pl.BlockSpec((tm, o.shape[1]), lambda i: (i, 0)),
                     _const_spec(wo.shape), _const_spec((1, d))]
    args += [gpre, gpost, wg, wu, wd]
    in_specs += [_const_spec((1, d)), _const_spec((1, d)),
                 _layer_spec(wg, layer), _layer_spec(wu, layer), _layer_spec(wd, layer)]
    body = functools.partial(_ffn_body, with_attn=attn is not None, ff_chunks=ff_chunks)
    return pl.pallas_call(
        body,
        grid=(n // tm,),
        in_specs=in_specs,
        out_specs=row_spec,
        out_shape=jax.ShapeDtypeStruct((n, d), F32),
        scratch_shapes=[pltpu.VMEM((tm, d_ff), BF16)],
        compiler_params=pltpu.CompilerParams(
            dimension_semantics=("arbitrary",),
            vmem_limit_bytes=VMEM_LIMIT_BYTES),
        name="attn_out_ffn" if attn is not None else "ffn",
    )(*args)


def _proj_body(x_ref, gkv_ref, gq_ref, wk_ref, wv_ref, wq_ref, k_ref, v_ref, *attn_refs,
               for_prompt):
    x = x_ref[...]
    src = _rms(x, gkv_ref[...], NORM_EPS).astype(BF16)
    v = jnp.dot(src, wv_ref[...], preferred_element_type=F32)
    v_ref[...] = v
    k = jnp.dot(src, wk_ref[...], preferred_element_type=F32)
    k_ref[...] = k
    h = _rms(x, gq_ref[...], NORM_EPS).astype(BF16)
    q = jnp.dot(h, wq_ref[...], preferred_element_type=F32) * (HEAD_DIM ** -0.5)
    if for_prompt:
        kb_ref, vtb_ref, qb_ref = attn_refs
        vtb_ref[...] = v.T.astype(BF16)
        kb_ref[...] = k.astype(BF16)
        qb_ref[...] = (q * LOG2E).astype(BF16)
    else:
        q_ref, = attn_refs
        q_ref[...] = q


def _proj(x, gkv, gq, wk, wv, wq, *, tm, for_prompt):
    n, d = x.shape
    w = wk.shape[1]
    row_in = pl.BlockSpec((tm, d), lambda i: (i, 0))
    row_out = pl.BlockSpec((tm, w), lambda i: (i, 0))
    col_out = pl.BlockSpec((w, tm), lambda i: (0, i))
    row_f32, row_bf16 = jax.ShapeDtypeStruct((n, w), F32), jax.ShapeDtypeStruct((n, w), BF16)
    if for_prompt:
        attn_specs = [row_out, col_out, row_out]
        attn_shapes = [row_bf16, jax.ShapeDtypeStruct((w, n), BF16), row_bf16]
    else:
        attn_specs, attn_shapes = [row_out], [row_f32]
    return pl.pallas_call(
        functools.partial(_proj_body, for_prompt=for_prompt),
        grid=(n // tm,),
        in_specs=[row_in, _const_spec((1, d)), _const_spec((1, d)),
                  _const_spec(wk.shape), _const_spec(wv.shape), _const_spec(wq.shape)],
        out_specs=[row_out, row_out, *attn_specs],
        out_shape=[row_f32, row_f32, *attn_shapes],
        compiler_params=pltpu.CompilerParams(
            dimension_semantics=("arbitrary",),
            vmem_limit_bytes=VMEM_LIMIT_BYTES),
        name="kvq_proj",
    )(x, gkv, gq, wk, wv, wq)


def _lambda(lq_ref, lam_init):
    lq = lq_ref[...]
    a = jnp.sum(lq[0:1] * lq[1:2], axis=-1, keepdims=True)
    b = jnp.sum(lq[2:3] * lq[3:4], axis=-1, keepdims=True)
    return jnp.exp(a) - jnp.exp(b) + lam_init


def _diff_finalize(o1, o2, lq_ref, gsub_ref, lam_init):
    od = o1 - _lambda(lq_ref, lam_init) * o2
    return _rms(od, gsub_ref[...], SUBLN_EPS) * (1.0 - lam_init)


def _online_update(m_ref, l_ref, acc_ref, m_cur):
    m_prev = m_ref[...]
    m_new = jnp.maximum(m_prev, m_cur)
    alpha = jnp.exp(m_prev - m_new)
    m_ref[...] = m_new
    l_ref[...] = alpha * l_ref[...]
    acc_ref[...] = alpha * acc_ref[...]
    return m_new


def _alibi_feature_rows(slopes):
    cs = slopes * LOG2E
    c1 = cs.astype(BF16).astype(F32)
    c2 = (cs - c1).astype(BF16).astype(F32)
    c3 = (cs - c1 - c2).astype(BF16).astype(F32)
    c = jnp.stack([c1, c2, c3], axis=1)
    half = N_POS_FEATURES // 2
    pad = jnp.zeros((slopes.shape[0], V_DIM - N_POS_FEATURES), F32)
    zeros = jnp.zeros((slopes.shape[0], half), F32)
    q_rows = jnp.concatenate([POS_RADIX * c, c, zeros, pad], axis=1)
    k_rows = jnp.concatenate([zeros, -POS_RADIX * c, -c, pad], axis=1)
    return q_rows, k_rows


def _prompt_phases(item, q_ref, k_ref, vt_ref, qrow_ref, krow_ref, lq_ref, gsub_ref, o_ref,
                   pos_ref, mask_ref, kaug_ref, qfeat_ref, s_ref, p_ref, *, blk, lam_init):
    seq = q_ref.shape[0]
    nq = seq // blk
    slot = item % nq
    head = (item // nq) % N_HEADS
    half = N_POS_FEATURES // 2
    contract_last = (((1,), (1,)), ((), ()))

    def init_tables():
        pos = lax.broadcasted_iota(jnp.int32, (seq, V_DIM), 0)
        lane = lax.broadcasted_iota(jnp.int32, (seq, V_DIM), 1)
        hi = (pos >> _log2(POS_RADIX)).astype(F32)
        lo = (pos & (POS_RADIX - 1)).astype(F32)
        third = half // 2
        is_hi = (lane < third) | ((lane >= half) & (lane < half + third))
        pos_ref[...] = jnp.where(lane >= N_POS_FEATURES, 0.0, jnp.where(is_hi, hi, lo))
        key = lax.broadcasted_iota(jnp.int32, (blk, 2 * blk), 0)
        qry = lax.broadcasted_iota(jnp.int32, (blk, 2 * blk), 1) & (blk - 1)
        mask_ref[...] = jnp.where(key <= qry, 0.0, NEG_INF)

    def prepare_head():
        lane = lax.broadcasted_iota(jnp.int32, (seq, V_DIM), 1)
        pos_feat = pos_ref[...]
        kaug_ref[:, 0:V_DIM] = k_ref[...]
        kaug_ref[:, V_DIM:] = jnp.where(lane < half, pos_feat,
                                        krow_ref[pl.ds(head, 1), :]).astype(BF16)
        qfeat_ref[...] = jnp.where(lane < half, qrow_ref[pl.ds(head, 1), :], pos_feat).astype(BF16)

    def query_block(i):
        lane = lax.broadcasted_iota(jnp.int32, (blk, V_DIM), 1)
        n_keys = (i + 1) * blk
        q = q_ref[i * blk:(i + 1) * blk, :]
        zero = jnp.zeros_like(q)
        qf = qfeat_ref[i * blk:(i + 1) * blk, :]
        q2 = jnp.concatenate(
            [jnp.concatenate([jnp.where(lane < HEAD_DIM, q, zero), qf], axis=1),
             jnp.concatenate([jnp.where(lane >= HEAD_DIM, q, zero), qf], axis=1)], axis=0)
        s_ref[0:n_keys, :] = lax.dot_general(kaug_ref[0:n_keys, :], q2, contract_last,
                                             preferred_element_type=F32)
        diag = slice(i * blk, (i + 1) * blk)
        s_ref[diag, :] = s_ref[diag, :] + mask_ref[...]
        m = None
        for j in range(i + 1):
            m_j = jnp.max(s_ref[j * blk:(j + 1) * blk, :], axis=0, keepdims=True)
            m = m_j if m is None else jnp.maximum(m, m_j)
        l = jnp.zeros_like(m)
        for j in range(i + 1):
            rows = slice(j * blk, (j + 1) * blk)
            p = jnp.exp2(s_ref[rows, :] - m)
            l = l + jnp.sum(p, axis=0, keepdims=True)
            p_ref[rows, :] = p.astype(BF16)
        ot = jnp.dot(vt_ref[:, 0:n_keys], p_ref[0:n_keys, :], preferred_element_type=F32) / l
        o = _diff_finalize(ot[:, :blk].T, ot[:, blk:].T, lq_ref, gsub_ref, lam_init)
        o_ref[i * blk:(i + 1) * blk, :] = o.astype(o_ref.dtype)

    def prepare():
        pl.when(item == 0)(init_tables)
        pl.when(slot == 0)(prepare_head)

    return prepare, slot, [functools.partial(query_block, i) for i in range(nq)]


def _prompt_scratch(seq, blk):
    return [
        pltpu.VMEM((seq, V_DIM), F32),
        pltpu.VMEM((blk, 2 * blk), F32),
        pltpu.VMEM((seq, 2 * V_DIM), BF16),
        pltpu.VMEM((seq, V_DIM), BF16),
        pltpu.VMEM((seq, 2 * blk), F32),
        pltpu.VMEM((seq, 2 * blk), BF16),
    ]


def _sample_phases(g, q_ref, qprev_ref, slope_ref, k_refs, v_refs, knew_ref, vnew_ref, lq_ref,
                   gsub_ref, o_ref, bias_ref, p_ref, mp_ref, lp_ref, m_ref, l_ref, acc_ref, *,
                   n_steps, n_groups, past_len, dec_seq, lam_init):
    pages_per_step = len(k_refs)
    prev_step = _value_group(g) % n_steps
    n_rows = q_ref.shape[1]
    slope = slope_ref[:, 0:1]
    contract_last = (((1,), (1,)), ((), ()))

    def row_col_ids(shape):
        r = lax.broadcasted_iota(jnp.int32, shape, 0)
        c = lax.broadcasted_iota(jnp.int32, shape, 1)
        head_q = (r >> _log2(dec_seq)) & (N_HEADS - 1)
        return head_q, r & (dec_seq - 1), c >> _log2(N_HEADS), c & (N_HEADS - 1)

    def init_tables():
        hq, tok, key, hk = row_col_ids((n_rows, PAGE_ROWS))
        bias_ref[...] = jnp.where(hq == hk, -slope * (tok - key).astype(F32), NEG_INF)
        p_ref[...] = jnp.zeros(p_ref.shape, p_ref.dtype)
        mp_ref[...] = jnp.full(mp_ref.shape, NEG_INF, F32)
        lp_ref[...] = jnp.zeros(lp_ref.shape, F32)

    def reset_row():
        m_ref[...] = jnp.full(m_ref.shape, NEG_INF, F32)
        l_ref[...] = jnp.zeros(l_ref.shape, F32)
        acc_ref[...] = jnp.zeros(acc_ref.shape, F32)

    def prepare():
        pl.when(g == 0)(init_tables)
        pl.when(prev_step == 0)(reset_row)

    def body():
        m_cur = mp_ref[0]
        for i in range(1, pages_per_step):
            m_cur = jnp.maximum(m_cur, mp_ref[i])
        m_new = _online_update(m_ref, l_ref, acc_ref, m_cur)
        l_add = jnp.zeros(l_ref.shape, F32)
        acc_add = jnp.zeros(acc_ref.shape, F32)
        for i in range(pages_per_step):
            w = jnp.exp(mp_ref[i] - m_new)
            l_add = l_add + w * lp_ref[i]
            acc_add = acc_add + w * jnp.dot(p_ref[i], v_refs[i][...], preferred_element_type=F32)
        l_ref[...] += l_add
        acc_ref[...] += acc_add

        q = q_ref[0]
        bias = bias_ref[...]
        step = _score_group(g, n_groups) % n_steps
        for i in range(pages_per_step):
            s = lax.dot_general(q, k_refs[i][...], contract_last, preferred_element_type=F32)
            t = s + bias
            m_page = jnp.max(t, axis=-1, keepdims=True)
            p = jnp.exp(t - m_page)
            p_ref[i] = p.astype(p_ref.dtype)
            lp_ref[i] = jnp.sum(p, axis=-1, keepdims=True)
            page_pos = (step * pages_per_step + i) * PAGE_SIZE
            mp_ref[i] = m_page - slope * (past_len - page_pos).astype(F32)

    def finish_row():
        qp = qprev_ref[0]
        hq, tok, key, hk = row_col_ids((n_rows, knew_ref.shape[1]))
        valid = (hq == hk) & (key <= tok)
        bias_new = jnp.where(valid, -slope * (tok - key).astype(F32), NEG_INF)
        s = lax.dot_general(qp, knew_ref[0], contract_last, preferred_element_type=F32)
        t = s + bias_new
        m_fin = _online_update(m_ref, l_ref, acc_ref, jnp.max(t, axis=-1, keepdims=True))
        p = jnp.exp(t - m_fin)
        l = l_ref[...] + jnp.sum(p, axis=-1, keepdims=True)
        acc = acc_ref[...] + jnp.dot(p, vnew_ref[0], preferred_element_type=F32)
        o = acc / l
        half = n_rows // 2
        o_ref[0] = _diff_finalize(o[:half], o[half:], lq_ref, gsub_ref, lam_init)

    def finish():
        pl.when((g > 0) & (prev_step == n_steps - 1))(finish_row)

    return prepare, body, finish


def _score_group(g, n_groups):
    return jnp.minimum(g, n_groups - 1)


def _value_group(g):
    return jnp.maximum(g - 1, 0)


def _attention_body(pt_ref, q_ref, qprev_ref, slope_ref, ck_ref, cv_ref, knew_ref, vnew_ref, lq_ref,
                    gsub_ref, qb_ref, kb_ref, vtb_ref, qrow_ref, krow_ref, os_ref, op_ref,
                    kbuf_ref, vbuf_ref, ksem, vsem, *scratch, pages_per_step, n_groups, n_items,
                    blk, sample_args, lam_init):
    sample_scratch = scratch[:N_SAMPLE_SCRATCH]
    prompt_scratch = scratch[N_SAMPLE_SCRATCH:]
    g = pl.program_id(0)

    def page_copies(cache_ref, buf_ref, sem, group):
        slot = group % PAGE_RING
        copies = []
        for i in range(pages_per_step):
            row0 = pl.multiple_of(pt_ref[group * pages_per_step + i] * PAGE_ROWS, PAGE_ROWS)
            copies.append(pltpu.make_async_copy(
                cache_ref.at[pl.ds(row0, PAGE_ROWS), :], buf_ref.at[slot, i], sem.at[slot]))
        return copies

    def start_keys(group):
        for i, copy in enumerate(page_copies(ck_ref, kbuf_ref, ksem, group)):
            copy.start(priority=i % 2)

    def start_values(group):
        for i, copy in enumerate(page_copies(cv_ref, vbuf_ref, vsem, group)):
            copy.start(priority=i % 2)

    @pl.when(g == 0)
    def _():
        start_keys(0)
        start_values(0)
        if n_groups > 1:
            start_keys(1)
        vbuf_ref[PAGE_RING - 1] = jnp.zeros(vbuf_ref.shape[1:], vbuf_ref.dtype)

    pl.when(g + 2 < n_groups)(lambda: start_keys(g + 2))
    pl.when(g + 1 < n_groups)(lambda: start_values(g + 1))

    @pl.when(g < n_groups)
    def _():
        for copy in page_copies(ck_ref, kbuf_ref, ksem, g):
            copy.wait()

    @pl.when(g > 0)
    def _():
        for copy in page_copies(cv_ref, vbuf_ref, vsem, g - 1):
            copy.wait()

    key_slot = g % PAGE_RING
    value_slot = (g + PAGE_RING - 1) % PAGE_RING
    k_refs = [kbuf_ref.at[key_slot, i] for i in range(pages_per_step)]
    v_refs = [vbuf_ref.at[value_slot, i] for i in range(pages_per_step)]
    sample_prepare, sample_body, sample_finish = _sample_phases(
        g, q_ref, qprev_ref, slope_ref, k_refs, v_refs, knew_ref, vnew_ref, lq_ref, gsub_ref,
        os_ref, *sample_scratch, lam_init=lam_init, **sample_args)
    has_item = g < n_items
    prompt_prepare, slot, slot_bodies = _prompt_phases(
        jnp.minimum(g, n_items - 1), qb_ref, kb_ref, vtb_ref, qrow_ref, krow_ref, lq_ref,
        gsub_ref, op_ref, *prompt_scratch, blk=blk, lam_init=lam_init)

    sample_prepare()
    pl.when(has_item)(prompt_prepare)

    def both(slot_body):
        sample_body()
        slot_body()

    for j, slot_body in enumerate(slot_bodies):
        pl.when(has_item & (slot == j))(functools.partial(both, slot_body))
    pl.when(jnp.logical_not(has_item))(sample_body)
    sample_finish()


def _attention(qb, kb, vtb, q_s, k_s, v_s, cache_k, cache_v, page_table, slopes, lq, gsub, *,
               batch, seq, lam_init):
    dec_batch, dec_seq, _ = q_s.shape
    n_pages = page_table.shape[1]
    pps = PAGES_PER_STEP
    n_steps = n_pages // pps
    n_groups = dec_batch * n_steps
    n_rows = 2 * N_HEADS * dec_seq
    blk = ATTN_BLOCK
    n_slots = seq // blk
    n_items = batch * N_HEADS * n_slots
    assert n_items <= n_groups + 1, (n_items, n_groups)

    q5 = q_s.reshape(dec_batch, dec_seq, N_HEADS, 2, HEAD_DIM)
    zeros = jnp.zeros_like(q5[..., 0, :])
    qm = jnp.stack([jnp.concatenate([q5[..., 0, :], zeros], axis=-1),
                    jnp.concatenate([zeros, q5[..., 1, :]], axis=-1)], axis=1)
    qall = qm.transpose(0, 1, 3, 2, 4).reshape(dec_batch, n_rows, V_DIM)
    row_slope = jnp.broadcast_to(jnp.tile(jnp.repeat(slopes, dec_seq), 2)[:, None], (n_rows, 128))

    def pad_new(a):
        a = a.reshape(dec_batch, dec_seq, N_HEADS, V_DIM)
        a = jnp.pad(a, ((0, 0), (0, NEW_KEY_PAD - dec_seq), (0, 0), (0, 0)))
        return a.reshape(dec_batch, NEW_KEY_PAD * N_HEADS, V_DIM)

    knew, vnew = pad_new(k_s), pad_new(v_s)
    ck = cache_k.reshape(-1, V_DIM)
    cv = cache_v.reshape(-1, V_DIM)
    q_rows, k_rows = _alibi_feature_rows(slopes)

    score_row = lambda g, pt: (_score_group(g, n_groups) // n_steps, 0, 0)
    value_row = lambda g, pt: (_value_group(g) // n_steps, 0, 0)
    const2 = lambda g, pt: (0, 0)

    def pair(g):
        p = jnp.minimum(g, n_items - 1) // n_slots
        return p // N_HEADS, p % N_HEADS

    head_rows = pl.BlockSpec((seq, V_DIM), lambda g, pt: pair(g))
    head_cols = pl.BlockSpec((V_DIM, seq), lambda g, pt: pair(g)[::-1])
    cache_spec = pl.BlockSpec(memory_space=pl.ANY)
    page_ring = pltpu.VMEM((PAGE_RING, pps, PAGE_ROWS, V_DIM), cache_k.dtype)
    sample_scratch = [
        pltpu.VMEM((n_rows, PAGE_ROWS), F32),
        pltpu.VMEM((pps, n_rows, PAGE_ROWS), BF16),
        pltpu.VMEM((pps, n_rows, 1), F32),
        pltpu.VMEM((pps, n_rows, 1), F32),
        pltpu.VMEM((n_rows, 1), F32),
        pltpu.VMEM((n_rows, 1), F32),
        pltpu.VMEM((n_rows, V_DIM), F32),
    ]
    assert len(sample_scratch) == N_SAMPLE_SCRATCH
    body = functools.partial(
        _attention_body, pages_per_step=pps, n_groups=n_groups, n_items=n_items, blk=blk,
        lam_init=lam_init,
        sample_args=dict(n_steps=n_steps, n_groups=n_groups, past_len=n_pages * PAGE_SIZE,
                         dec_seq=dec_seq))
    grid_spec = pltpu.PrefetchScalarGridSpec(
        num_scalar_prefetch=1,
        grid=(n_groups + 1,),
        in_specs=[
            pl.BlockSpec((1, n_rows, V_DIM), score_row),
            pl.BlockSpec((1, n_rows, V_DIM), value_row),
            pl.BlockSpec(row_slope.shape, const2),
            cache_spec, cache_spec,
            pl.BlockSpec((1,) + knew.shape[1:], value_row),
            pl.BlockSpec((1,) + vnew.shape[1:], value_row),
            pl.BlockSpec(lq.shape, const2),
            pl.BlockSpec(gsub.shape, const2),
            head_rows, head_rows, head_cols,
            pl.BlockSpec(q_rows.shape, const2),
            pl.BlockSpec(k_rows.shape, const2),
        ],
        out_specs=[pl.BlockSpec((1, n_rows // 2, V_DIM), value_row), head_rows],
        scratch_shapes=[
            page_ring, page_ring,
            pltpu.SemaphoreType.DMA((PAGE_RING,)), pltpu.SemaphoreType.DMA((PAGE_RING,)),
            *sample_scratch, *_prompt_scratch(seq, blk),
        ],
    )
    o_s, o_p = pl.pallas_call(
        body,
        grid_spec=grid_spec,
        out_shape=[jax.ShapeDtypeStruct((dec_batch, n_rows // 2, V_DIM), F32),
                   jax.ShapeDtypeStruct(qb.shape, BF16)],
        compiler_params=pltpu.CompilerParams(
            dimension_semantics=("arbitrary",),
            vmem_limit_bytes=VMEM_LIMIT_BYTES),
        name="diff_attention",
    )(page_table.reshape(-1), qall, qall, row_slope, ck, cv,
      knew, vnew, lq, gsub, qb, kb, vtb, q_rows, k_rows)
    o_s = o_s.reshape(dec_batch, N_HEADS, dec_seq, V_DIM).transpose(0, 2, 1, 3)
    return o_p, o_s.reshape(dec_batch * dec_seq, N_HEADS * V_DIM)


def kernel(x_prompt, x_sample, state_pool, cache_k, cache_v, page_table, w_pool, pool_scale, g_kv,
           w_k, w_v, w_q, w_o, lambda_qk, g_sub, g_mix_pre, g_mix_post, g_ffn_pre, g_ffn_post,
           w_gate, w_up, w_down):
    batch, seq, d = x_prompt.shape
    dec_batch, dec_seq, _ = x_sample.shape
    past_len = page_table.shape[1] * PAGE_SIZE
    depth = g_mix_pre.shape[0]
    assert depth == 2 and w_pool.shape[0] == N_A_LAYERS and d == D_MODEL

    row = lambda a: a.reshape(1, -1)
    wpool = w_pool[0].astype(BF16)
    wg, wu, wd = (w.astype(BF16) for w in (w_gate, w_up, w_down))
    wk, wv, wq, wo = (w.astype(BF16) for w in (w_k, w_v, w_q[0], w_o[0]))
    slopes = jnp.exp2(-8.0 * (jnp.arange(N_HEADS, dtype=F32) + 1.0) / N_HEADS)
    lam_init = 0.8 - 0.6 * math.exp(-0.3 * 1)
    lq = lambda_qk[0].astype(F32)
    gsub = row(g_sub[0])

    def layer0(x, prev, pos0, tt, tm, rows=1):
        b, t, _ = x.shape
        x1, state = _mixer(x, prev, wpool, row(pool_scale[0]), row(g_mix_pre[0]),
                           row(g_mix_post[0]), pos0=pos0, tt=tt, rows=rows)
        x2 = _ffn(x1.reshape(b * t, d), row(g_ffn_pre[0]), row(g_ffn_post[0]),
                  wg, wu, wd, layer=0, tm=tm)
        return x2, state

    def layer1_tail(x2, o, tm):
        return _ffn(x2, row(g_ffn_pre[1]), row(g_ffn_post[1]), wg, wu, wd, layer=1, tm=tm,
                    attn=(o, wo, row(g_mix_post[1])))

    zero_prev = jnp.zeros((batch, POOL_STATE_ROWS, d), F32)
    x2p, state_p = _mixer_ffn(x_prompt, zero_prev, wpool, row(pool_scale[0]), row(g_mix_pre[0]),
                              row(g_mix_post[0]), row(g_ffn_pre[0]), row(g_ffn_post[0]),
                              wg, wu, wd, layer=0, pos0=0, tt=TOKEN_TILE)
    x2p = x2p.reshape(batch * seq, d)
    kp, vp, kpb, vtpb, qpb = _proj(x2p, row(g_kv), row(g_mix_pre[1]), wk, wv, wq, tm=TOKEN_TILE,
                                   for_prompt=True)
    n_s = dec_batch * dec_seq
    x2s, state_s = layer0(x_sample, state_pool[0], past_len, dec_seq, n_s, rows=SAMPLE_MIXER_ROWS)
    ks, vs, qs = _proj(x2s, row(g_kv), row(g_mix_pre[1]), wk, wv, wq, tm=n_s, for_prompt=False)

    op, os_ = _attention(qpb, kpb, vtpb, qs.reshape(dec_batch, dec_seq, -1),
                         ks.reshape(dec_batch, dec_seq, -1), vs.reshape(dec_batch, dec_seq, -1),
                         cache_k, cache_v, page_table, slopes, lq, gsub,
                         batch=batch, seq=seq, lam_init=lam_init)
    y_prompt = layer1_tail(x2p, op, TOKEN_TILE).reshape(batch, seq, d)
    y_sample = layer1_tail(x2s, os_, n_s).reshape(dec_batch, dec_seq, d)

    kv_shape_p = (batch, seq, N_HEADS, V_DIM)
    kv_shape_s = (dec_batch, dec_seq, N_HEADS, V_DIM)
    return (y_prompt, y_sample, state_p[None], state_s[None],
            kp.reshape(kv_shape_p), vp.reshape(kv_shape_p),
            ks.reshape(kv_shape_s), vs.reshape(kv_shape_s))
```
